```python
import jax, jax.numpy as jnp
from jax import lax
import numpy as np

D_MODEL = 1024
BATCH = 8
SEQ = 8192
DEPTH = 2
DEC_BATCH = 32
DEC_SEQ = 16
PAST_LEN = 4096

CHUNK = 64
Q_BLOCK = 128
HEAD_DIM = 64
SB_HEADS = 8
SB_WIDTH = SB_HEADS * HEAD_DIM
CONV_CH = D_MODEL // 2
CONV_WIDTH = 31
L0_IN = 3 * SB_WIDTH + 2 * CONV_CH
DSA_HEADS = 16
DSA_KV_HEADS = 4
DSA_GROUP = DSA_HEADS // DSA_KV_HEADS
IDX_HEADS = 8
IDX_DIM = 64
TOPK_MAX = 256
L1_IN = DSA_HEADS * HEAD_DIM + 2 * DSA_KV_HEADS * HEAD_DIM + IDX_HEADS * IDX_DIM + IDX_DIM + IDX_HEADS
ROPE_DIM = HEAD_DIM // 4
ROPE_THETA = 500000.0
FF_DENSE = 2816
N_EXPERTS = 8
TOP_K_EXPERTS = 2
FF_EXPERT = 3584
EPS = 1e-6
NEG = -1e30

kernel_name = 'hybrid_stickbreak_conformer_dsa_stream_step'


def rms_norm(x, g):
    xf = x.astype(jnp.float32)
    y = xf * lax.rsqrt(jnp.mean(xf * xf, axis=-1, keepdims=True) + EPS)
    return (y * g.astype(jnp.float32)).astype(x.dtype)


def layer_norm(x, g, b):
    xf = x.astype(jnp.float32)
    mu = jnp.mean(xf, axis=-1, keepdims=True)
    var = jnp.mean(jnp.square(xf - mu), axis=-1, keepdims=True)
    y = (xf - mu) * lax.rsqrt(var + EPS) * g.astype(jnp.float32) + b.astype(jnp.float32)
    return y.astype(x.dtype)


def ada_mod(c, w, b):
    m = jax.nn.silu(c) @ w + b
    return jnp.split(m[:, None, :], 6, axis=-1)


def partial_rope(x, pos):
    half = ROPE_DIM // 2
    inv = 1.0 / (ROPE_THETA ** (jnp.arange(half, dtype=jnp.float32) / half))
    ang = pos.astype(jnp.float32)[:, None] * inv[None, :]
    cos = jnp.cos(ang)[None, :, None, :]
    sin = jnp.sin(ang)[None, :, None, :]
    xr = x[..., :ROPE_DIM].astype(jnp.float32)
    x1, x2 = xr[..., :half], xr[..., half:]
    rot = jnp.concatenate([x1 * cos - x2 * sin, x1 * sin + x2 * cos], axis=-1).astype(x.dtype)
    return jnp.concatenate([rot, x[..., ROPE_DIM:]], axis=-1)


def map_query_blocks(fn, qpos, *xs):
    L = qpos.shape[0]
    blk = Q_BLOCK if L % Q_BLOCK == 0 else L
    nb = L // blk

    def split(a):
        return a.reshape(a.shape[0], nb, blk, *a.shape[2:]).swapaxes(0, 1)

    out = lax.map(lambda a: fn(*a), (qpos.reshape(nb, blk),) + tuple(split(a) for a in xs))
    out = out.swapaxes(0, 1)
    return out.reshape(out.shape[0], L, *out.shape[3:])


def stick_breaking_block(qpos, q, k, v, kpos):
    z = jnp.einsum('bqhd,bshd->bhqs', q, k).astype(jnp.float32) * (HEAD_DIM ** -0.5)
    mask = (kpos[None, :] < qpos[:, None])[None, None]
    log_beta = jax.nn.log_sigmoid(z)
    log_1m = jnp.where(mask, jax.nn.log_sigmoid(-z), 0.0)
    between = lax.cumsum(log_1m, axis=3, reverse=True) - log_1m
    w = jnp.where(mask, jnp.exp(log_beta + between), 0.0)
    return jnp.einsum('bhqs,bshd->bqhd', w.astype(v.dtype), v)


def depthwise_causal_conv(x_pad, w, b):
    out = lax.conv_general_dilated(x_pad, w[:, None, :].astype(x_pad.dtype), window_strides=(1,), padding='VALID',
                                   dimension_numbers=('NWC', 'WIO', 'NWC'), feature_group_count=x_pad.shape[-1])
    return out + b


def dsa_block(qpos, q, qi, wi, k, v, ki, kpos, topk):
    adm = (kpos[None, :] // CHUNK) <= (qpos[:, None] // CHUNK)
    idx_logits = jnp.einsum('bqhe,bse->bqhs', qi, ki).astype(jnp.float32) * (IDX_DIM ** -0.5)
    score = jnp.einsum('bqh,bqhs->bqs', wi.astype(jnp.float32), jax.nn.relu(idx_logits))
    score = jnp.where(adm[None], score, NEG)
    _, sel = lax.top_k(score, topk)
    valid = (kpos[sel] // CHUNK) <= (qpos[None, :, None] // CHUNK)
    kg = jax.vmap(lambda kk, ii: kk[ii])(k, sel)
    vg = jax.vmap(lambda vv, ii: vv[ii])(v, sel)
    s = jnp.einsum('bqhgd,bqkhd->bqhgk', q, kg).astype(jnp.float32) * (HEAD_DIM ** -0.5)
    s = jnp.where(valid[:, :, None, None, :], s, NEG)
    p = jax.nn.softmax(s, axis=-1)
    return jnp.einsum('bqhgk,bqkhd->bqhgd', p.astype(vg.dtype), vg)


def mixer_sb_conv(h, pos, past, W):
    B, L, _ = h.shape
    q, k, v, u_val, u_gate = jnp.split(h @ W['l0_w_in'], [SB_WIDTH, 2 * SB_WIDTH, 3 * SB_WIDTH, 3 * SB_WIDTH + CONV_CH], axis=-1)
    q = q.reshape(B, L, SB_HEADS, HEAD_DIM)
    k = k.reshape(B, L, SB_HEADS, HEAD_DIM)
    v = v.reshape(B, L, SB_HEADS, HEAD_DIM)
    if past is None:
        k_all, v_all = k, v
        conv_prev = jnp.zeros((B, CONV_WIDTH - 1, CONV_CH), h.dtype)
    else:
        k_all = jnp.concatenate([past[0], k], axis=1)
        v_all = jnp.concatenate([past[1], v], axis=1)
        conv_prev = past[2]
    kpos = jnp.arange(k_all.shape[1], dtype=jnp.int32)
    o_a = map_query_blocks(lambda qp, qb: stick_breaking_block(qp, qb, k_all, v_all, kpos), pos, q)
    u = u_val * jax.nn.sigmoid(u_gate)
    u_pad = jnp.concatenate([conv_prev, u], axis=1)
    conv = depthwise_causal_conv(u_pad, W['l0_conv_w'], W['l0_conv_b'])
    o_b = jax.nn.silu(layer_norm(conv, W['l0_conv_ln_g'], W['l0_conv_ln_b']))
    out = jnp.concatenate([o_a.reshape(B, L, SB_WIDTH), o_b], axis=-1) @ W['l0_w_out']
    return out, k, v, u_pad[:, -(CONV_WIDTH - 1):, :]


def mixer_dsa(h, pos, past, W):
    B, L, _ = h.shape
    cuts = np.cumsum([DSA_HEADS * HEAD_DIM, DSA_KV_HEADS * HEAD_DIM, DSA_KV_HEADS * HEAD_DIM, IDX_HEADS * IDX_DIM, IDX_DIM]).tolist()
    q, k, v, qi, ki, wi = jnp.split(h @ W['l1_w_in'], cuts, axis=-1)
    q = partial_rope(rms_norm(q.reshape(B, L, DSA_HEADS, HEAD_DIM), W['l1_q_norm']), pos)
    k = partial_rope(rms_norm(k.reshape(B, L, DSA_KV_HEADS, HEAD_DIM), W['l1_k_norm']), pos)
    v = v.reshape(B, L, DSA_KV_HEADS, HEAD_DIM)
    qi = partial_rope(qi.reshape(B, L, IDX_HEADS, IDX_DIM), pos)
    ki = partial_rope(ki[:, :, None, :], pos)[:, :, 0, :]
    wi = wi * (IDX_HEADS ** -0.5)
    if past is None:
        k_all, v_all, ki_all = k, v, ki
    else:
        k_all = jnp.concatenate([past[0], k], axis=1)
        v_all = jnp.concatenate([past[1], v], axis=1)
        ki_all = jnp.concatenate([past[2], ki], axis=1)
    S = k_all.shape[1]
    kpos = jnp.arange(S, dtype=jnp.int32)
    topk = min(TOPK_MAX, S // 4)
    qg = q.reshape(B, L, DSA_KV_HEADS, DSA_GROUP, HEAD_DIM)
    o = map_query_blocks(lambda qp, qb, qib, wib: dsa_block(qp, qb, qib, wib, k_all, v_all, ki_all, kpos, topk), pos, qg, qi, wi)
    out = o.reshape(B, L, DSA_HEADS * HEAD_DIM) @ W['l1_w_out']
    return out, k, v, ki


def swiglu(h, wg, wu, wd):
    return (jax.nn.silu(h @ wg) * (h @ wu)) @ wd


def moe_swiglu(h, router, wg, wu, wd):
    logits = (h @ router).astype(jnp.float32)
    top_v, top_i = lax.top_k(logits, TOP_K_EXPERTS)
    gates = jax.nn.softmax(top_v, axis=-1)
    dense_gate = jnp.sum(jax.nn.one_hot(top_i, N_EXPERTS, dtype=jnp.float32) * gates[..., None], axis=-2)
    out = jnp.zeros_like(h)
    for e in range(N_EXPERTS):
        out = out + dense_gate[..., e:e + 1].astype(h.dtype) * swiglu(h, wg[e], wu[e], wd[e])
    return out


def trunk(x, c, past, W):
    L = x.shape[1]
    past_len = 0 if past is None else past[0].shape[1]
    pos = past_len + jnp.arange(L, dtype=jnp.int32)
    states = []
    for layer in range(DEPTH):
        if layer % 2 == 0:
            sh_m, sc_m, g_m, sh_f, sc_f, g_f = ada_mod(c, W['l0_ada_w'], W['l0_ada_b'])
            h = rms_norm(x, W['l0_norm_mix']) * (1.0 + sc_m) + sh_m
            mix, sb_k, sb_v, conv_state = mixer_sb_conv(h, pos, None if past is None else past[0:3], W)
            x = x + g_m * mix
            h = rms_norm(x, W['l0_norm_ff']) * (1.0 + sc_f) + sh_f
            x = x + g_f * swiglu(h, W['l0_ff_wg'], W['l0_ff_wu'], W['l0_ff_wd'])
            states += [sb_k, sb_v, conv_state]
        else:
            sh_m, sc_m, g_m, sh_f, sc_f, g_f = ada_mod(c, W['l1_ada_w'], W['l1_ada_b'])
            h = rms_norm(x, W['l1_norm_mix']) * (1.0 + sc_m) + sh_m
            mix, dk, dv, dki = mixer_dsa(h, pos, None if past is None else past[3:6], W)
            x = x + g_m * mix
            h = rms_norm(x, W['l1_norm_ff']) * (1.0 + sc_f) + sh_f
            x = x + g_f * moe_swiglu(h, W['l1_router'], W['l1_exp_wg'], W['l1_exp_wu'], W['l1_exp_wd'])
            states += [dk, dv, dki]
    return x, states


def setup_inputs(seed: int = 0) -> dict:
    key = jax.random.key(seed)
    ks = iter(jax.random.split(key, 40))
    D = D_MODEL

    def nrm(shape, scale):
        return jax.random.normal(next(ks), shape, jnp.float32) * scale

    def gain(n):
        return 1.0 + nrm((n,), 0.02)

    inp = {}
    inp['x_prompt'] = nrm((BATCH, SEQ, D), 1.0)
    inp['x_sample'] = nrm((DEC_BATCH, DEC_SEQ, D), 1.0)
    inp['c_prompt'] = nrm((BATCH, D), 1.0)
    inp['c_sample'] = nrm((DEC_BATCH, D), 1.0)
    inp['cache_sb_k'] = nrm((DEC_BATCH, PAST_LEN, SB_HEADS, HEAD_DIM), 1.0)
    inp['cache_sb_v'] = nrm((DEC_BATCH, PAST_LEN, SB_HEADS, HEAD_DIM), 1.0)
    inp['cache_conv'] = nrm((DEC_BATCH, CONV_WIDTH - 1, CONV_CH), 0.5)
    inp['cache_dsa_k'] = nrm((DEC_BATCH, PAST_LEN, DSA_KV_HEADS, HEAD_DIM), 1.0)
    inp['cache_dsa_v'] = nrm((DEC_BATCH, PAST_LEN, DSA_KV_HEADS, HEAD_DIM), 1.0)
    inp['cache_dsa_kidx'] = nrm((DEC_BATCH, PAST_LEN, IDX_DIM), 1.0)
    inp['l0_ada_w'] = nrm((D, 6 * D), 0.5 * D ** -0.5)
    inp['l0_ada_b'] = nrm((6 * D,), 0.02)
    inp['l0_norm_mix'] = gain(D)
    inp['l0_w_in'] = nrm((D, L0_IN), D ** -0.5)
    inp['l0_conv_w'] = nrm((CONV_WIDTH, CONV_CH), CONV_WIDTH ** -0.5)
    inp['l0_conv_b'] = nrm((CONV_CH,), 0.02)
    inp['l0_conv_ln_g'] = gain(CONV_CH)
    inp['l0_conv_ln_b'] = nrm((CONV_CH,), 0.02)
    inp['l0_w_out'] = nrm((SB_WIDTH + CONV_CH, D), (SB_WIDTH + CONV_CH) ** -0.5)
    inp['l0_norm_ff'] = gain(D)
    inp['l0_ff_wg'] = nrm((D, FF_DENSE), D ** -0.5)
    inp['l0_ff_wu'] = nrm((D, FF_DENSE), D ** -0.5)
    inp['l0_ff_wd'] = nrm((FF_DENSE, D), FF_DENSE ** -0.5)
    inp['l1_ada_w'] = nrm((D, 6 * D), 0.5 * D ** -0.5)
    inp['l1_ada_b'] = nrm((6 * D,), 0.02)
    inp['l1_norm_mix'] = gain(D)
    inp['l1_w_in'] = nrm((D, L1_IN), D ** -0.5)
    inp['l1_q_norm'] = gain(HEAD_DIM)
    inp['l1_k_norm'] = gain(HEAD_DIM)
    inp['l1_w_out'] = nrm((DSA_HEADS * HEAD_DIM, D), (DSA_HEADS * HEAD_DIM) ** -0.5)
    inp['l1_norm_ff'] = gain(D)
    inp['l1_router'] = nrm((D, N_EXPERTS), D ** -0.5)
    inp['l1_exp_wg'] = nrm((N_EXPERTS, D, FF_EXPERT), D ** -0.5)
    inp['l1_exp_wu'] = nrm((N_EXPERTS, D, FF_EXPERT), D ** -0.5)
    inp['l1_exp_wd'] = nrm((N_EXPERTS, FF_EXPERT, D), FF_EXPERT ** -0.5)
    return inp


def reference(x_prompt, x_sample, c_prompt, c_sample, cache_sb_k, cache_sb_v, cache_conv, cache_dsa_k, cache_dsa_v,
              cache_dsa_kidx, l0_ada_w, l0_ada_b, l0_norm_mix, l0_w_in, l0_conv_w, l0_conv_b, l0_conv_ln_g,
              l0_conv_ln_b, l0_w_out, l0_norm_ff, l0_ff_wg, l0_ff_wu, l0_ff_wd, l1_ada_w, l1_ada_b, l1_norm_mix,
              l1_w_in, l1_q_norm, l1_k_norm, l1_w_out, l1_norm_ff, l1_router, l1_exp_wg, l1_exp_wu, l1_exp_wd):
    W = dict(l0_ada_w=l0_ada_w, l0_ada_b=l0_ada_b, l0_norm_mix=l0_norm_mix, l0_w_in=l0_w_in, l0_conv_w=l0_conv_w,
             l0_conv_b=l0_conv_b, l0_conv_ln_g=l0_conv_ln_g, l0_conv_ln_b=l0_conv_ln_b, l0_w_out=l0_w_out,
             l0_norm_ff=l0_norm_ff, l0_ff_wg=l0_ff_wg, l0_ff_wu=l0_ff_wu, l0_ff_wd=l0_ff_wd, l1_ada_w=l1_ada_w,
             l1_ada_b=l1_ada_b, l1_norm_mix=l1_norm_mix, l1_w_in=l1_w_in, l1_q_norm=l1_q_norm, l1_k_norm=l1_k_norm,
             l1_w_out=l1_w_out, l1_norm_ff=l1_norm_ff, l1_router=l1_router, l1_exp_wg=l1_exp_wg,
             l1_exp_wu=l1_exp_wu, l1_exp_wd=l1_exp_wd)
    y_prompt, st_p = trunk(x_prompt, c_prompt, None, W)
    y_sample, st_s = trunk(x_sample, c_sample,
                           (cache_sb_k, cache_sb_v, cache_conv, cache_dsa_k, cache_dsa_v, cache_dsa_kidx), W)
    sb_k_p, sb_v_p, conv_p, dsa_k_p, dsa_v_p, dsa_kidx_p = st_p
    sb_k_s, sb_v_s, conv_s, dsa_k_s, dsa_v_s, dsa_kidx_s = st_s
    return (y_prompt, y_sample, sb_k_p, sb_v_p, conv_p, dsa_k_p, dsa_v_p, dsa_kidx_p,
            sb_k_s, sb_v_s, conv_s, dsa_k_s, dsa_v_s, dsa_kidx_s)
```

```python
import functools
import math

import numpy as np
import jax
import jax.numpy as jnp
from jax import lax
from jax.experimental import pallas as pl
from jax.experimental.pallas import tpu as pltpu

F32 = jnp.float32
BF16 = jnp.bfloat16
I32 = jnp.int32

EPS = 1e-6
NEG = -1e30
HEAD_DIM = 64
CHUNK = 64
TOPK_MAX = 256
ROPE_DIM = 16
ROPE_THETA = 500000.0
CONV_WIDTH = 31
SB_HEADS = 8
DSA_HEADS = 16
DSA_KV_HEADS = 4
IDX_HEADS = 8
N_EXPERTS = 8
LANES = 128
INT_MIN = -2 ** 31
VMEM_LIMIT = 56 * 2 ** 20


def _cparams(sem):
    return pltpu.CompilerParams(dimension_semantics=sem, vmem_limit_bytes=VMEM_LIMIT)


def _dot(a, b):
    return jnp.dot(a, b, preferred_element_type=F32)


def _dot_nt(a, b):
    return lax.dot_general(a, b, (((1,), (1,)), ((), ())), preferred_element_type=F32)


def _split2(x):
    hi = x.astype(BF16)
    lo = (x - hi.astype(F32)).astype(BF16)
    return hi, lo


def _sigmoid(x):
    return 1.0 / (1.0 + jnp.exp(-x))


def _modnorm(x, g, sc, sh):
    ms = jnp.mean(x * x, axis=-1, keepdims=True)
    y = x * lax.rsqrt(ms + EPS) * g
    return y * (1.0 + sc) + sh


def _mod_spec(mod, tm):
    if mod.shape[1] == 1:
        return pl.BlockSpec((None, 1, mod.shape[2]), lambda b, i, *_: (b, 0, 0))
    return pl.BlockSpec((None, tm, mod.shape[2]), lambda b, i, *_: (b, i, 0))


def _row_tile(L, pref):
    return pref if L % pref == 0 else L


def _ada_kernel(c_ref, w_ref, b_ref, o_ref):
    c = c_ref[...]
    s = c * _sigmoid(c)
    s_hi, s_lo = _split2(s)
    w_hi, w_lo = _split2(w_ref[...])
    o_ref[...] = _dot(s_hi, w_hi) + _dot(s_lo, w_hi) + _dot(s_hi, w_lo) + b_ref[...]


def _ada(c, w, b):
    bc, d = c.shape
    n = w.shape[1]
    tn = 512
    return pl.pallas_call(
        _ada_kernel,
        grid=(n // tn,),
        in_specs=[pl.BlockSpec((bc, d), lambda j: (0, 0)),
                  pl.BlockSpec((d, tn), lambda j: (0, j)),
                  pl.BlockSpec((1, tn), lambda j: (0, j))],
        out_specs=pl.BlockSpec((bc, tn), lambda j: (0, j)),
        out_shape=jax.ShapeDtypeStruct((bc, n), F32),
        compiler_params=_cparams(("arbitrary",)),
        name="ada_mod",
    )(c, w, b.reshape(1, n))


def _proj0_kernel(x_ref, g_ref, sc_ref, sh_ref, w_ref, q_ref, k_ref, v_ref, kb_ref, vb_ref, u_ref):
    h = _modnorm(x_ref[...], g_ref[...], sc_ref[...], sh_ref[...]).astype(BF16)
    wd = q_ref.shape[-1]

    def mm(c):
        return _dot(h, w_ref[:, c * wd:(c + 1) * wd])

    q_ref[...] = (mm(0) * (HEAD_DIM ** -0.5)).astype(BF16)
    k = mm(1)
    k_ref[...] = k
    kb_ref[...] = k.astype(BF16)
    v = mm(2)
    v_ref[...] = v
    vb_ref[...] = v.astype(BF16)
    u_ref[...] = mm(3) * _sigmoid(mm(4))


def _proj0(x, g, sc, sh, w):
    B, L, D = x.shape
    wd = SB_HEADS * HEAD_DIM
    tm = _row_tile(L, 512)
    row = lambda d, dt: (pl.BlockSpec((None, tm, d), lambda b, i: (b, i, 0)), jax.ShapeDtypeStruct((B, L, d), dt))
    outs = [row(wd, BF16), row(wd, F32), row(wd, F32), row(wd, BF16), row(wd, BF16), row(wd, F32)]
    return pl.pallas_call(
        _proj0_kernel,
        grid=(B, L // tm),
        in_specs=[pl.BlockSpec((None, tm, D), lambda b, i: (b, i, 0)),
                  pl.BlockSpec((1, D), lambda b, i: (0, 0)),
                  _mod_spec(sc, tm), _mod_spec(sh, tm),
                  pl.BlockSpec(w.shape, lambda b, i: (0, 0))],
        out_specs=[o[0] for o in outs],
        out_shape=[o[1] for o in outs],
        compiler_params=_cparams(("arbitrary", "arbitrary")),
        name="l0_in_proj",
    )(x, g.reshape(1, D), sc, sh, w)


def _sb_kernel(q_ref, k_ref, v_ref, o_ref, *, tq, tk, past, nkb_max):
    i = pl.program_id(2)
    qbase = past + i * tq
    nkb = jnp.minimum((qbase + tq - 2 + tk) // tk, nkb_max)
    lane = lax.broadcasted_iota(I32, (1, LANES), 1)
    half = [lane < HEAD_DIM, lane >= HEAD_DIM]
    q = q_ref[...]
    qm = [jnp.where(half[hh], q, jnp.zeros_like(q)) for hh in range(2)]
    rr = lax.broadcasted_iota(I32, (tk, tk), 0)
    cc = lax.broadcasted_iota(I32, (tk, tk), 1)
    U = jnp.where(rr > cc, 1.0, 0.0).astype(BF16)
    dcol = lax.broadcasted_iota(I32, (tq, tk), 1) - lax.broadcasted_iota(I32, (tq, tk), 0)

    def body(jj, carry):
        r0, r1, acc = carry
        j = nkb - 1 - jj
        k0 = pl.multiple_of(j * tk, tk)
        kb = k_ref[pl.ds(k0, tk), :]
        vb = v_ref[pl.ds(k0, tk), :]
        mask = dcol < (qbase - k0)
        rs = [r0, r1]
        for hh in range(2):
            z = _dot_nt(qm[hh], kb)
            sp = jnp.log(1.0 + jnp.exp(-jnp.abs(z)))
            lb = jnp.minimum(z, 0.0) - sp
            l1m = -jnp.maximum(z, 0.0) - sp
            l1m = jnp.where(mask, l1m, 0.0)
            hi, lo = _split2(l1m)
            cs = _dot(hi, U) + _dot(lo, U)
            w = jnp.where(mask, jnp.exp(lb + cs + rs[hh]), 0.0)
            vm = jnp.where(half[hh], vb, jnp.zeros_like(vb))
            acc = acc + _dot(w.astype(BF16), vm)
            rs[hh] = rs[hh] + cs[:, 0:1] + l1m[:, 0:1]
        return rs[0], rs[1], acc

    z1 = jnp.zeros((tq, 1), F32)
    _, _, acc = lax.fori_loop(0, nkb, body, (z1, z1, jnp.zeros((tq, LANES), F32)))
    o_ref[...] = acc.astype(o_ref.dtype)


def _sb_attention(q, k, v, past, tq, tk):
    B, L, W = q.shape
    S = k.shape[1]
    hp = W // LANES
    kern = functools.partial(_sb_kernel, tq=tq, tk=tk, past=past, nkb_max=S // tk)
    return pl.pallas_call(
        kern,
        grid=(B, hp, L // tq),
        in_specs=[pl.BlockSpec((None, tq, LANES), lambda b, h, i: (b, i, h)),
                  pl.BlockSpec((None, S, LANES), lambda b, h, i: (b, 0, h)),
                  pl.BlockSpec((None, S, LANES), lambda b, h, i: (b, 0, h))],
        out_specs=pl.BlockSpec((None, tq, LANES), lambda b, h, i: (b, i, h)),
        out_shape=jax.ShapeDtypeStruct((B, L, W), BF16),
        compiler_params=_cparams(("arbitrary", "arbitrary", "arbitrary")),
        name="stickbreak_attn",
    )(q, k, v)


def _conv_kernel(*refs, tm, has_halo):
    if has_halo:
        u_ref, halo_ref, prev_ref, w_ref, b_ref, g_ref, be_ref, o_ref, buf_ref, cv_ref = refs
    else:
        u_ref, prev_ref, w_ref, b_ref, g_ref, be_ref, o_ref, buf_ref, cv_ref = refs
    i = pl.program_id(1)
    pad = 32
    if has_halo:
        buf_ref[0:pad, :] = jnp.where(i == 0, prev_ref[...], halo_ref[...])
    else:
        buf_ref[0:pad, :] = prev_ref[...]
    buf_ref[pad:pad + tm, :] = u_ref[...]
    C = u_ref.shape[-1]
    rt = min(tm, 64)
    off = pad - (CONV_WIDTH - 1)
    for c in range(C // LANES):
        cs = slice(c * LANES, (c + 1) * LANES)
        for r in range(tm // rt):
            acc = jnp.zeros((rt, LANES), F32)
            for j in range(CONV_WIDTH):
                acc = acc + buf_ref[r * rt + off + j: r * rt + off + j + rt, cs] * w_ref[j:j + 1, cs]
            cv_ref[r * rt:(r + 1) * rt, cs] = acc
    conv = cv_ref[...] + b_ref[...]
    mu = jnp.mean(conv, axis=-1, keepdims=True)
    xc = conv - mu
    var = jnp.mean(xc * xc, axis=-1, keepdims=True)
    y = xc * lax.rsqrt(var + EPS) * g_ref[...] + be_ref[...]
    o_ref[...] = (y * _sigmoid(y)).astype(o_ref.dtype)


def _conv_module(u, prev32, w, b, g, be):
    B, L, C = u.shape
    tm = _row_tile(L, 256)
    has_halo = L > tm
    kern = functools.partial(_conv_kernel, tm=tm, has_halo=has_halo)
    in_specs = [pl.BlockSpec((None, tm, C), lambda b_, i: (b_, i, 0))]
    args = [u]
    if has_halo:
        r = tm // 32
        in_specs.append(pl.BlockSpec((None, 32, C), lambda b_, i: (b_, jnp.maximum(i * r - 1, 0), 0)))
        args.append(u)
    vec = pl.BlockSpec((1, C), lambda b_, i: (0, 0))
    in_specs += [pl.BlockSpec((None, 32, C), lambda b_, i: (b_, 0, 0)),
                 pl.BlockSpec((32, C), lambda b_, i: (0, 0)), vec, vec, vec]
    wpad = jnp.concatenate([w, jnp.zeros((32 - CONV_WIDTH, C), F32)], axis=0)
    args += [prev32, wpad, b.reshape(1, C), g.reshape(1, C), be.reshape(1, C)]
    return pl.pallas_call(
        kern,
        grid=(B, L // tm),
        in_specs=in_specs,
        out_specs=pl.BlockSpec((None, tm, C), lambda b_, i: (b_, i, 0)),
        out_shape=jax.ShapeDtypeStruct((B, L, C), BF16),
        scratch_shapes=[pltpu.VMEM((tm + 32, C), F32), pltpu.VMEM((tm, C), F32)],
        compiler_params=_cparams(("arbitrary", "arbitrary")),
        name="conv_module",
    )(*args)


def _outproj_kernel(*refs, n_in):
    x_ref, g_ref = refs[0], refs[1]
    a_refs = refs[2:2 + n_in]
    w_refs = refs[2 + n_in:2 + 2 * n_in]
    o_ref = refs[2 + 2 * n_in]
    acc = _dot(a_refs[0][...], w_refs[0][...])
    for a_ref, w_ref in zip(a_refs[1:], w_refs[1:]):
        acc = acc + _dot(a_ref[...], w_ref[...])
    o_ref[...] = x_ref[...] + g_ref[...] * acc


def _outproj(x, gate, acts, ws):
    B, L, D = x.shape
    tm = _row_tile(L, 512)
    n_in = len(acts)
    in_specs = [pl.BlockSpec((None, tm, D), lambda b, i: (b, i, 0)), _mod_spec(gate, tm)]
    in_specs += [pl.BlockSpec((None, tm, a.shape[-1]), lambda b, i: (b, i, 0)) for a in acts]
    in_specs += [pl.BlockSpec(w.shape, lambda b, i: (0, 0)) for w in ws]
    return pl.pallas_call(
        functools.partial(_outproj_kernel, n_in=n_in),
        grid=(B, L // tm),
        in_specs=in_specs,
        out_specs=pl.BlockSpec((None, tm, D), lambda b, i: (b, i, 0)),
        out_shape=jax.ShapeDtypeStruct((B, L, D), F32),
        compiler_params=_cparams(("arbitrary", "arbitrary")),
        name="out_proj",
    )(x, gate, *acts, *ws)


def _ffn_kernel(x_ref, g_ref, sc_ref, sh_ref, gate_ref, wg_ref, wu_ref, wd_ref, o_ref, h_ref, acc_ref):
    f = pl.program_id(2)

    @pl.when(f == 0)
    def _():
        h_ref[...] = _modnorm(x_ref[...], g_ref[...], sc_ref[...], sh_ref[...]).astype(BF16)
        acc_ref[...] = jnp.zeros_like(acc_ref)

    h = h_ref[...]
    a = _dot(h, wg_ref[...])
    a = a * _sigmoid(a) * _dot(h, wu_ref[...])
    acc_ref[...] += _dot(a.astype(BF16), wd_ref[...])

    @pl.when(f == pl.num_programs(2) - 1)
    def _():
        o_ref[...] = x_ref[...] + gate_ref[...] * acc_ref[...]


def _ffn(x, g, sc, sh, gate, wg, wu, wd, tf):
    B, L, D = x.shape
    F = wg.shape[1]
    tm = _row_tile(L, 512)
    return pl.pallas_call(
        _ffn_kernel,
        grid=(B, L // tm, F // tf),
        in_specs=[pl.BlockSpec((None, tm, D), lambda b, i, f: (b, i, 0)),
                  pl.BlockSpec((1, D), lambda b, i, f: (0, 0)),
                  _mod_spec(sc, tm), _mod_spec(sh, tm), _mod_spec(gate, tm),
                  pl.BlockSpec((D, tf), lambda b, i, f: (0, f)),
                  pl.BlockSpec((D, tf), lambda b, i, f: (0, f)),
                  pl.BlockSpec((tf, D), lambda b, i, f: (f, 0))],
        out_specs=pl.BlockSpec((None, tm, D), lambda b, i, f: (b, i, 0)),
        out_shape=jax.ShapeDtypeStruct((B, L, D), F32),
        scratch_shapes=[pltpu.VMEM((tm, D), BF16), pltpu.VMEM((tm, D), F32)],
        compiler_params=_cparams(("arbitrary", "arbitrary", "arbitrary")),
        name="dense_swiglu",
    )(x, g.reshape(1, D), sc, sh, gate, wg, wu, wd)


def _moe_kernel(x_ref, g_ref, sc_ref, sh_ref, gate_ref, r_ref, wg_ref, wu_ref, wd_ref, o_ref,
                h_ref, acc_ref, dg_ref):
    e = pl.program_id(2)
    f = pl.program_id(3)
    lane = lax.broadcasted_iota(I32, (1, LANES), 1)

    @pl.when((e == 0) & (f == 0))
    def _():
        h = _modnorm(x_ref[...], g_ref[...], sc_ref[...], sh_ref[...]).astype(BF16)
        h_ref[...] = h
        acc_ref[...] = jnp.zeros_like(acc_ref)
        logits = jnp.where(lane < N_EXPERTS, _dot(h, r_ref[...]), -jnp.inf)
        m1 = jnp.max(logits, axis=-1, keepdims=True)
        i1 = jnp.min(jnp.where(logits == m1, lane, LANES), axis=-1, keepdims=True)
        rest = jnp.where(lane == i1, -jnp.inf, logits)
        m2 = jnp.max(rest, axis=-1, keepdims=True)
        i2 = jnp.min(jnp.where(rest == m2, lane, LANES), axis=-1, keepdims=True)
        e2 = jnp.exp(m2 - m1)
        den = 1.0 + e2
        dg_ref[...] = jnp.where(lane == i1, 1.0 / den, 0.0) + jnp.where(lane == i2, e2 / den, 0.0)

    h = h_ref[...]
    a = _dot(h, wg_ref[...])
    a = a * _sigmoid(a) * _dot(h, wu_ref[...])
    y = _dot(a.astype(BF16), wd_ref[...])
    ge = jnp.sum(jnp.where(lane == e, dg_ref[...], 0.0), axis=-1, keepdims=True)
    acc_ref[...] += ge * y

    @pl.when((e == pl.num_programs(2) - 1) & (f == pl.num_programs(3) - 1))
    def _():
        o_ref[...] = x_ref[...] + gate_ref[...] * acc_ref[...]


def _moe(x, g, sc, sh, gate, router, wg, wu, wd, tf):
    B, L, D = x.shape
    E, _, F = wg.shape
    tm = _row_tile(L, 512)
    return pl.pallas_call(
        _moe_kernel,
        grid=(B, L // tm, E, F // tf),
        in_specs=[pl.BlockSpec((None, tm, D), lambda b, i, e, f: (b, i, 0)),
                  pl.BlockSpec((1, D), lambda b, i, e, f: (0, 0)),
                  _mod_spec(sc, tm), _mod_spec(sh, tm), _mod_spec(gate, tm),
                  pl.BlockSpec(router.shape, lambda b, i, e, f: (0, 0)),
                  pl.BlockSpec((None, D, tf), lambda b, i, e, f: (e, 0, f)),
                  pl.BlockSpec((None, D, tf), lambda b, i, e, f: (e, 0, f)),
                  pl.BlockSpec((None, tf, D), lambda b, i, e, f: (e, f, 0))],
        out_specs=pl.BlockSpec((None, tm, D), lambda b, i, e, f: (b, i, 0)),
        out_shape=jax.ShapeDtypeStruct((B, L, D), F32),
        scratch_shapes=[pltpu.VMEM((tm, D), BF16), pltpu.VMEM((tm, D), F32), pltpu.VMEM((tm, LANES), F32)],
        compiler_params=_cparams(("arbitrary",) * 4),
        name="expert_swiglu",
    )(x, g.reshape(1, D), sc, sh, gate, router, wg, wu, wd)


_P1_Q = 0
_P1_K = 1024
_P1_V = 1280
_P1_QI = 1536
_P1_KD = 2048
_P1_VD = 2560
_P1_KIWI = 3072
_P1_KID = 3200
_P1_N = 3328


def _rope(a, c, s1, s2):
    return a * c + pltpu.roll(a, LANES - ROPE_DIM // 2, 1) * s1 + pltpu.roll(a, ROPE_DIM // 2, 1) * s2


def _proj1_kernel(x_ref, g_ref, sc_ref, sh_ref, w_ref, c_ref, s1_ref, s2_ref, qg_ref, kg_ref,
                  q_ref, k_ref, v_ref, qi_ref, kd_ref, vd_ref, kiwi_ref, kid_ref):
    h = _modnorm(x_ref[...], g_ref[...], sc_ref[...], sh_ref[...]).astype(BF16)
    cos, s1, s2 = c_ref[...], s1_ref[...], s2_ref[...]
    rr = lax.broadcasted_iota(I32, (LANES, LANES), 0) // HEAD_DIM
    cc = lax.broadcasted_iota(I32, (LANES, LANES), 1) // HEAD_DIM
    bd = jnp.where(rr == cc, 1.0, 0.0).astype(BF16)
    lane = lax.broadcasted_iota(I32, (1, LANES), 1)

    def headnorm(a, gain):
        hi, lo = _split2(a * a)
        ss = _dot(hi, bd) + _dot(lo, bd)
        return a * lax.rsqrt(ss * (1.0 / HEAD_DIM) + EPS) * gain

    def group(c0, width):
        y = _dot(h, w_ref[:, c0:c0 + width])
        return [y[:, i * LANES:(i + 1) * LANES] for i in range(width // LANES)]

    qscale = HEAD_DIM ** -0.5
    for gidx in range(2):
        for i, a in enumerate(group(_P1_Q + gidx * 512, 512)):
            a = _rope(headnorm(a, qg_ref[...]), cos, s1, s2)
            cidx = gidx * 4 + i
            q_ref[:, cidx * LANES:(cidx + 1) * LANES] = (a * qscale).astype(BF16)
    kv = group(_P1_K, 512)
    for i in range(2):
        k_ref[:, i * LANES:(i + 1) * LANES] = _rope(headnorm(kv[i], kg_ref[...]), cos, s1, s2)
        v_ref[:, i * LANES:(i + 1) * LANES] = kv[2 + i]
    for i, a in enumerate(group(_P1_QI, 512)):
        qi_ref[:, i * LANES:(i + 1) * LANES] = (_rope(a, cos, s1, s2) * (HEAD_DIM ** -0.5)).astype(BF16)
    for i, a in enumerate(group(_P1_KD, 512)):
        kd_ref[:, i * LANES:(i + 1) * LANES] = _rope(headnorm(a, kg_ref[...]), cos, s1, s2).astype(BF16)
    for i, a in enumerate(group(_P1_VD, 512)):
        vd_ref[:, i * LANES:(i + 1) * LANES] = a.astype(BF16)
    kiwi, kid = group(_P1_KIWI, 256)
    first = lane < HEAD_DIM
    kiwi = _rope(kiwi, jnp.where(first, cos, 1.0), jnp.where(first, s1, 0.0), jnp.where(first, s2, 0.0))
    is_wi = (lane >= HEAD_DIM) & (lane < HEAD_DIM + IDX_HEADS)
    kiwi_ref[...] = kiwi * jnp.where(is_wi, IDX_HEADS ** -0.5, 1.0)
    kid_ref[...] = _rope(kid, cos, s1, s2).astype(BF16)


def _proj1(x, g, sc, sh, w, rope_tabs, qg, kg):
    B, L, D = x.shape
    tm = _row_tile(L, 512)
    row = lambda d, dt: (pl.BlockSpec((None, tm, d), lambda b, i: (b, i, 0)), jax.ShapeDtypeStruct((B, L, d), dt))
    outs = [row(1024, BF16), row(256, F32), row(256, F32), row(512, BF16), row(512, BF16), row(512, BF16),
            row(LANES, F32), row(LANES, BF16)]
    tab = pl.BlockSpec((tm, LANES), lambda b, i: (i, 0))
    vec = pl.BlockSpec((1, LANES), lambda b, i: (0, 0))
    return pl.pallas_call(
        _proj1_kernel,
        grid=(B, L // tm),
        in_specs=[pl.BlockSpec((None, tm, D), lambda b, i: (b, i, 0)),
                  pl.BlockSpec((1, D), lambda b, i: (0, 0)),
                  _mod_spec(sc, tm), _mod_spec(sh, tm),
                  pl.BlockSpec(w.shape, lambda b, i: (0, 0)),
                  tab, tab, tab, vec, vec],
        out_specs=[o[0] for o in outs],
        out_shape=[o[1] for o in outs],
        compiler_params=_cparams(("arbitrary", "arbitrary")),
        name="l1_in_proj",
    )(x, g.reshape(1, D), sc, sh, w, *rope_tabs, qg, kg)


def _pad_l1_weight(w):
    D = w.shape[0]
    q, k, v, qi, ki, wi = jnp.split(w, [1024, 1280, 1536, 2048, 2112], axis=1)
    dup = lambda m, nh: jnp.repeat(m.reshape(D, nh, 1, HEAD_DIM), 2, axis=2).reshape(D, nh * 2 * HEAD_DIM)
    kiwi = jnp.concatenate([ki, wi, jnp.zeros((D, LANES - HEAD_DIM - IDX_HEADS), w.dtype)], axis=1)
    out = jnp.concatenate([q, k, v, qi, dup(k, DSA_KV_HEADS), dup(v, DSA_KV_HEADS), kiwi, dup(ki, 1)], axis=1)
    assert out.shape[1] == _P1_N
    return out.astype(BF16)


def _rope_tables(pos):
    half = ROPE_DIM // 2
    inv = 1.0 / (ROPE_THETA ** (jnp.arange(half, dtype=F32) / half))
    ang = pos.astype(F32)[:, None] * inv[None, :]
    cos, sin = jnp.cos(ang), jnp.sin(ang)
    n = pos.shape[0]
    one = jnp.ones((n, HEAD_DIM - ROPE_DIM), F32)
    zero = jnp.zeros((n, HEAD_DIM - ROPE_DIM), F32)
    z8 = jnp.zeros((n, half), F32)
    c = jnp.concatenate([cos, cos, one], axis=1)
    s1 = jnp.concatenate([-sin, z8, zero], axis=1)
    s2 = jnp.concatenate([z8, sin, zero], axis=1)
    return tuple(jnp.tile(t, (1, 2)) for t in (c, s1, s2))


def _sort_key(score):
    b = lax.bitcast_convert_type(score + 0.0, I32)
    return jnp.where(b < 0, b ^ 0x7FFFFFFF, b)


def _sort_key_const(value):
    b = int(np.float32(value).view(np.int32))
    return b ^ 0x7FFFFFFF if b < 0 else b


def _topk_kernel(qi_ref, kiwi_ref, kid_ref, mask_ref, key_ref, *, tq, tk, past, s_valid, topk, nkb_max):
    i = pl.program_id(1)
    qbase = past + i * tq
    qrow = qbase + lax.broadcasted_iota(I32, (tq, 1), 0)
    adm_lim = jnp.minimum(((qrow >> 6) + 1) << 6, s_valid)
    lim = jnp.minimum((((qbase + tq - 1) >> 6) + 1) << 6, s_valid)
    nkb = jnp.minimum((lim + tk - 1) // tk, nkb_max)
    n_out = (jnp.zeros((tq, 1), I32) + (s_valid - jnp.minimum(nkb * tk, s_valid))).astype(F32)
    lane = lax.broadcasted_iota(I32, (1, LANES), 1)
    half = [lane < HEAD_DIM, lane >= HEAD_DIM]
    col = lax.broadcasted_iota(I32, (tq, tk), 1)
    negkey = _sort_key_const(NEG)
    kiwi = kiwi_ref[...]
    wi = [kiwi[:, HEAD_DIM + h:HEAD_DIM + h + 1] for h in range(IDX_HEADS)]
    qi = qi_ref[...]
    qim = []
    for h in range(IDX_HEADS):
        qc = qi[:, (h // 2) * LANES:(h // 2 + 1) * LANES]
        qim.append(jnp.where(half[h % 2], qc, jnp.zeros_like(qc)))

    def score_body(j, _):
        k0 = pl.multiple_of(j * tk, tk)
        kb = kid_ref[pl.ds(k0, tk), :]
        sc = jnp.zeros((tq, tk), F32)
        for h in range(IDX_HEADS):
            sc = sc + wi[h] * jnp.maximum(_dot_nt(qim[h], kb), 0.0)
        kpos = col + k0
        key = _sort_key(jnp.where(kpos < adm_lim, sc, NEG))
        key_ref[j] = jnp.where(kpos < s_valid, key, INT_MIN)
        return 0

    lax.fori_loop(0, nkb, score_body, 0)

    def count(thr, strict):
        def cbody(j, acc):
            kk = key_ref[j]
            for c in range(tk // LANES):
                kc = kk[:, c * LANES:(c + 1) * LANES]
                hit = (kc > thr) if strict else (kc >= thr)
                acc = acc + jnp.where(hit, 1.0, 0.0)
            return acc
        acc = lax.fori_loop(0, nkb, cbody, jnp.zeros((tq, LANES), F32))
        cnt = jnp.sum(acc, axis=-1, keepdims=True)
        out_hit = (negkey > thr) if strict else (negkey >= thr)
        return cnt + jnp.where(out_hit, n_out, 0.0)

    kf = float(topk)

    def bis_body(p, prefix):
        cand = prefix | jnp.left_shift(jnp.int32(1), 31 - p)
        cnt = count(cand ^ INT_MIN, False)
        return jnp.where(cnt >= kf, cand, prefix)

    prefix = lax.fori_loop(0, 32, bis_body, jnp.zeros((tq, 1), I32))
    tau = prefix ^ INT_MIN
    need = kf - count(tau, True)

    rr = lax.broadcasted_iota(I32, (tk, tk), 0)
    cc = lax.broadcasted_iota(I32, (tk, tk), 1)
    U = jnp.where(rr < cc, 1.0, 0.0).astype(BF16)

    def sel_body(j, carry):
        kk = key_ref[j]
        eq = jnp.where(kk == tau, 1.0, 0.0)
        rank = carry + _dot(eq.astype(BF16), U)
        take = jnp.where(kk > tau, 1.0, jnp.where(rank < need, eq, 0.0))
        kpos = col + j * tk
        mask_ref[j] = jnp.where(kpos < adm_lim, take, 0.0).astype(mask_ref.dtype)
        return carry + jnp.sum(eq, axis=-1, keepdims=True)

    lax.fori_loop(0, nkb, sel_body, jnp.zeros((tq, 1), F32))

    def zero_body(j, _):
        mask_ref[j] = jnp.zeros((tq, tk), mask_ref.dtype)
        return 0

    lax.fori_loop(nkb, nkb_max, zero_body, 0)


def _topk_mask(qi, kiwi, kid, past, s_valid, tq, tk):
    B, L, _ = qi.shape
    S = kid.shape[1]
    nb = S // tk
    topk = min(TOPK_MAX, s_valid // 4)
    kern = functools.partial(_topk_kernel, tq=tq, tk=tk, past=past, s_valid=s_valid, topk=topk, nkb_max=nb)
    return pl.pallas_call(
        kern,
        grid=(B, L // tq),
        in_specs=[pl.BlockSpec((None, tq, qi.shape[-1]), lambda b, i: (b, i, 0)),
                  pl.BlockSpec((None, tq, LANES), lambda b, i: (b, i, 0)),
                  pl.BlockSpec((None, S, LANES), lambda b, i: (b, 0, 0))],
        out_specs=pl.BlockSpec((None, None, nb, tq, tk), lambda b, i: (b, i, 0, 0, 0)),
        out_shape=jax.ShapeDtypeStruct((B, L // tq, nb, tq, tk), BF16),
        scratch_shapes=[pltpu.VMEM((nb, tq, tk), I32)],
        compiler_params=_cparams(("arbitrary", "arbitrary")),
        name="indexer_topk",
    )(qi, kiwi, kid)


def _dsa_kernel(q_ref, kd_ref, vd_ref, mask_ref, o_ref, m_ref, l_ref, acc_ref, *, tq, tk, past, s_valid, nkb_max):
    i = pl.program_id(1)
    qbase = past + i * tq
    lim = jnp.minimum((((qbase + tq - 1) >> 6) + 1) << 6, s_valid)
    nkb = jnp.minimum((lim + tk - 1) // tk, nkb_max)
    lane = lax.broadcasted_iota(I32, (1, LANES), 1)
    first = lane < HEAD_DIM
    for j in range(DSA_KV_HEADS):
        qs = []
        for c in (2 * j, 2 * j + 1):
            qc = q_ref[:, c * LANES:(c + 1) * LANES]
            qs += [jnp.where(first, qc, jnp.zeros_like(qc)), jnp.where(first, jnp.zeros_like(qc), qc)]
        q4 = jnp.concatenate(qs, axis=0)
        m_ref[...] = jnp.full(m_ref.shape, NEG, F32)
        l_ref[...] = jnp.zeros_like(l_ref)
        acc_ref[...] = jnp.zeros_like(acc_ref)

        def body(jb, _):
            k0 = pl.multiple_of(jb * tk, tk)
            kb = kd_ref[pl.ds(k0, tk), j * LANES:(j + 1) * LANES]
            vb = vd_ref[pl.ds(k0, tk), j * LANES:(j + 1) * LANES]
            mk = mask_ref[jb].astype(F32)
            mk4 = jnp.concatenate([mk] * 4, axis=0)
            s = jnp.where(mk4 > 0.5, _dot_nt(q4, kb), -jnp.inf)
            m_old = m_ref[...]
            m_new = jnp.maximum(m_old, jnp.max(s, axis=-1, keepdims=True))
            alpha = jnp.exp(m_old - m_new)
            p = jnp.exp(s - m_new)
            l_ref[...] = alpha * l_ref[...] + jnp.sum(p, axis=-1, keepdims=True)
            acc_ref[...] = alpha * acc_ref[...] + _dot(p.astype(BF16), vb)
            m_ref[...] = m_new
            return 0

        lax.fori_loop(0, nkb, body, 0)
        o4 = acc_ref[...] / l_ref[...]
        for cc in range(2):
            oc = jnp.where(first, o4[(2 * cc) * tq:(2 * cc + 1) * tq], o4[(2 * cc + 1) * tq:(2 * cc + 2) * tq])
            c = 2 * j + cc
            o_ref[:, c * LANES:(c + 1) * LANES] = oc.astype(o_ref.dtype)


def _dsa_attention(q, kd, vd, mask, past, s_valid, tq, tk):
    B, L, W = q.shape
    S = kd.shape[1]
    nb = S // tk
    kern = functools.partial(_dsa_kernel, tq=tq, tk=tk, past=past, s_valid=s_valid, nkb_max=nb)
    return pl.pallas_call(
        kern,
        grid=(B, L // tq),
        in_specs=[pl.BlockSpec((None, tq, W), lambda b, i: (b, i, 0)),
                  pl.BlockSpec((None, S, kd.shape[-1]), lambda b, i: (b, 0, 0)),
                  pl.BlockSpec((None, S, vd.shape[-1]), lambda b, i: (b, 0, 0)),
                  pl.BlockSpec((None, None, nb, tq, tk), lambda b, i: (b, i, 0, 0, 0))],
        out_specs=pl.BlockSpec((None, tq, W), lambda b, i: (b, i, 0)),
        out_shape=jax.ShapeDtypeStruct((B, L, W), BF16),
        scratch_shapes=[pltpu.VMEM((4 * tq, 1), F32), pltpu.VMEM((4 * tq, 1), F32), pltpu.VMEM((4 * tq, LANES), F32)],
        compiler_params=_cparams(("arbitrary", "arbitrary")),
        name="sparse_attn",
    )(q, kd, vd, mask)


def _pad_rows(a, mult):
    s = a.shape[1]
    sp = -(-s // mult) * mult
    if sp == s:
        return a
    return jnp.concatenate([a, jnp.zeros((a.shape[0], sp - s) + a.shape[2:], a.dtype)], axis=1)


def _dup_heads(a):
    B, S, H, d = a.shape
    return jnp.repeat(a[:, :, :, None, :], 2, axis=3).reshape(B, S, H * 2 * d)


def _trunk(x, mods0, mods1, past, W, flat):
    B, L, D = x.shape
    past_len = 0 if past is None else past[0].shape[1]

    def tok(a):
        return a.reshape(1, B * L, a.shape[-1]) if flat else a

    def untok(a):
        return a.reshape(B, L, a.shape[-1]) if flat else a

    def mod(m):
        if flat:
            return jnp.repeat(m, L, axis=0).reshape(1, B * L, D)
        return m[:, None, :]

    sh_m, sc_m, g_m, sh_f, sc_f, g_f = [mod(m) for m in mods0]
    xt = tok(x)

    q, k, v, kb, vb, u = _proj0(xt, W['l0_norm_mix'], sc_m, sh_m, W['l0_w_in'])
    q, k, v, kb, vb, u = [untok(a) for a in (q, k, v, kb, vb, u)]
    if past is None:
        tq = tk = min(L, 256)
        k_all, v_all = kb, vb
        prev = jnp.zeros((B, 32, u.shape[-1]), F32)
    else:
        tq, tk = L, 128
        k_all = _pad_rows(jnp.concatenate([past[0].reshape(B, past_len, -1).astype(BF16), kb], axis=1), tk)
        v_all = _pad_rows(jnp.concatenate([past[1].reshape(B, past_len, -1).astype(BF16), vb], axis=1), tk)
        prev = jnp.concatenate([jnp.zeros((B, 2, u.shape[-1]), F32), past[2]], axis=1)
    o_a = _sb_attention(q, k_all, v_all, past_len, tq, tk)
    o_b = _conv_module(u, prev, W['l0_conv_w'], W['l0_conv_b'], W['l0_conv_ln_g'], W['l0_conv_ln_b'])
    conv_state = jnp.concatenate([prev[:, 2:], u], axis=1)[:, -(CONV_WIDTH - 1):]
    xt = _outproj(xt, g_m, [tok(o_a), tok(o_b)], [W['l0_w_out'][:512], W['l0_w_out'][512:]])
    xt = _ffn(xt, W['l0_norm_ff'], sc_f, sh_f, g_f, W['l0_ff_wg'], W['l0_ff_wu'], W['l0_ff_wd'], 1408)

    sh_m, sc_m, g_m, sh_f, sc_f, g_f = [mod(m) for m in mods1]
    pos = past_len + jnp.arange(L, dtype=I32)
    tabs = _rope_tables(pos)
    if flat:
        tabs = tuple(jnp.tile(t, (B, 1)) for t in tabs)
    qg = jnp.tile(W['l1_q_norm'].reshape(1, HEAD_DIM), (1, 2))
    kg = jnp.tile(W['l1_k_norm'].reshape(1, HEAD_DIM), (1, 2))
    q1, k1, v1, qi, kd, vd, kiwi, kid = _proj1(xt, W['l1_norm_mix'], sc_m, sh_m, W['l1_w_in'], tabs, qg, kg)
    q1, k1, v1, qi, kd, vd, kiwi, kid = [untok(a) for a in (q1, k1, v1, qi, kd, vd, kiwi, kid)]
    s_valid = past_len + L
    tq1 = min(L, 128)
    tk1 = min(L, 512) if past is None else 512
    if past is not None:
        kd = _pad_rows(jnp.concatenate([_dup_heads(past[3]).astype(BF16), kd], axis=1), tk1)
        vd = _pad_rows(jnp.concatenate([_dup_heads(past[4]).astype(BF16), vd], axis=1), tk1)
        kid = _pad_rows(jnp.concatenate([_dup_heads(past[5][:, :, None, :]).astype(BF16), kid], axis=1), tk1)
    mask = _topk_mask(qi, kiwi, kid, past_len, s_valid, tq1, tk1)
    o1 = _dsa_attention(q1, kd, vd, mask, past_len, s_valid, tq1, tk1)
    xt = _outproj(xt, g_m, [tok(o1)], [W['l1_w_out']])
    xt = _moe(xt, W['l1_norm_ff'], sc_f, sh_f, g_f, W['l1_router'], W['l1_exp_wg'], W['l1_exp_wu'],
              W['l1_exp_wd'], 1792)

    hd = HEAD_DIM
    states = (k.reshape(B, L, SB_HEADS, hd), v.reshape(B, L, SB_HEADS, hd), conv_state,
              k1.reshape(B, L, DSA_KV_HEADS, hd), v1.reshape(B, L, DSA_KV_HEADS, hd), kiwi[..., :hd])
    return untok(xt), states


def kernel(x_prompt, x_sample, c_prompt, c_sample, cache_sb_k, cache_sb_v, cache_conv, cache_dsa_k, cache_dsa_v, cache_dsa_kidx, l0_ada_w, l0_ada_b, l0_norm_mix, l0_w_in, l0_conv_w, l0_conv_b, l0_conv_ln_g, l0_conv_ln_b, l0_w_out, l0_norm_ff, l0_ff_wg, l0_ff_wu, l0_ff_wd, l1_ada_w, l1_ada_b, l1_norm_mix, l1_w_in, l1_q_norm, l1_k_norm, l1_w_out, l1_norm_ff, l1_router, l1_exp_wg, l1_exp_wu, l1_exp_wd):
    D = x_prompt.shape[-1]
    router = jnp.concatenate([l1_router, jnp.zeros((D, LANES - N_EXPERTS), F32)], axis=1).astype(BF16)
    W = dict(l0_norm_mix=l0_norm_mix, l0_w_in=l0_w_in.astype(BF16), l0_conv_w=l0_conv_w, l0_conv_b=l0_conv_b,
             l0_conv_ln_g=l0_conv_ln_g, l0_conv_ln_b=l0_conv_ln_b, l0_w_out=l0_w_out.astype(BF16),
             l0_norm_ff=l0_norm_ff, l0_ff_wg=l0_ff_wg.astype(BF16), l0_ff_wu=l0_ff_wu.astype(BF16),
             l0_ff_wd=l0_ff_wd.astype(BF16), l1_norm_mix=l1_norm_mix, l1_w_in=_pad_l1_weight(l1_w_in),
             l1_q_norm=l1_q_norm, l1_k_norm=l1_k_norm, l1_w_out=l1_w_out.astype(BF16), l1_norm_ff=l1_norm_ff,
             l1_router=router, l1_exp_wg=l1_exp_wg.astype(BF16), l1_exp_wu=l1_exp_wu.astype(BF16),
             l1_exp_wd=l1_exp_wd.astype(BF16))
    bp = c_prompt.shape[0]
    c_all = jnp.concatenate([c_prompt, c_sample], axis=0)
    m0 = _ada(c_all, l0_ada_w, l0_ada_b)
    m1 = _ada(c_all, l1_ada_w, l1_ada_b)
    mods = lambda m, sl: [t[sl] for t in jnp.split(m, 6, axis=-1)]
    y_p, st_p = _trunk(x_prompt, mods(m0, slice(0, bp)), mods(m1, slice(0, bp)), None, W, flat=False)
    past = (cache_sb_k, cache_sb_v, cache_conv, cache_dsa_k, cache_dsa_v, cache_dsa_kidx)
    y_s, st_s = _trunk(x_sample, mods(m0, slice(bp, None)), mods(m1, slice(bp, None)), past, W, flat=True)
    return (y_p, y_s) + tuple(st_p) + tuple(st_s)
```

```python
import functools
import math

import numpy as np
import jax
import jax.numpy as jnp
from jax import lax
from jax.experimental import pallas as pl
from jax.experimental.pallas import tpu as pltpu

F32 = jnp.float32
BF16 = jnp.bfloat16
I32 = jnp.int32

EPS = 1e-6
NEG = -1e30
HEAD_DIM = 64
CHUNK = 64
TOPK_MAX = 256
ROPE_DIM = 16
ROPE_THETA = 500000.0
CONV_WIDTH = 31
SB_HEADS = 8
DSA_HEADS = 16
DSA_KV_HEADS = 4
IDX_HEADS = 8
N_EXPERTS = 8
LANES = 128
INT_MIN = -2 ** 31
VMEM_LIMIT = 56 * 2 ** 20


def _cparams(sem):
    return pltpu.CompilerParams(dimension_semantics=sem, vmem_limit_bytes=VMEM_LIMIT)


def _dot(a, b):
    return jnp.dot(a, b, preferred_element_type=F32)


def _dot_nt(a, b):
    return lax.dot_general(a, b, (((1,), (1,)), ((), ())), preferred_element_type=F32)


def _split2(x):
    hi = x.astype(BF16)
    lo = (x - hi.astype(F32)).astype(BF16)
    return hi, lo


def _sigmoid(x):
    return 1.0 / (1.0 + jnp.exp(-x))


def _modnorm(x, g, sc, sh):
    ms = jnp.mean(x * x, axis=-1, keepdims=True)
    y = x * lax.rsqrt(ms + EPS) * g
    return y * (1.0 + sc) + sh


def _mod_spec(mod, tm):
    if mod.shape[1] == 1:
        return pl.BlockSpec((None, 1, mod.shape[2]), lambda b, i, *_: (b, 0, 0))
    return pl.BlockSpec((None, tm, mod.shape[2]), lambda b, i, *_: (b, i, 0))


def _row_tile(L, pref):
    return pref if L % pref == 0 else L


def _ada_kernel(c_ref, w_ref, b_ref, o_ref):
    c = c_ref[...]
    s = c * _sigmoid(c)
    s_hi, s_lo = _split2(s)
    w_hi, w_lo = _split2(w_ref[...])
    o_ref[...] = _dot(s_hi, w_hi) + _dot(s_lo, w_hi) + _dot(s_hi, w_lo) + b_ref[...]


def _ada(c, w, b):
    bc, d = c.shape
    n = w.shape[1]
    tn = 512
    return pl.pallas_call(
        _ada_kernel,
        grid=(n // tn,),
        in_specs=[pl.BlockSpec((bc, d), lambda j: (0, 0)),
                  pl.BlockSpec((d, tn), lambda j: (0, j)),
                  pl.BlockSpec((1, tn), lambda j: (0, j))],
        out_specs=pl.BlockSpec((bc, tn), lambda j: (0, j)),
        out_shape=jax.ShapeDtypeStruct((bc, n), F32),
        compiler_params=_cparams(("arbitrary",)),
        name="ada_mod",
    )(c, w, b.reshape(1, n))


def _proj0_kernel(x_ref, g_ref, sc_ref, sh_ref, w_ref, q_ref, k_ref, v_ref, kb_ref, vb_ref, u_ref):
    h = _modnorm(x_ref[...], g_ref[...], sc_ref[...], sh_ref[...]).astype(BF16)
    wd = q_ref.shape[-1]

    def mm(c):
        return _dot(h, w_ref[:, c * wd:(c + 1) * wd])

    q_ref[...] = (mm(0) * (HEAD_DIM ** -0.5)).astype(BF16)
    k = mm(1)
    k_ref[...] = k
    kb_ref[...] = k.astype(BF16)
    v = mm(2)
    v_ref[...] = v
    vb_ref[...] = v.astype(BF16)
    u_ref[...] = mm(3) * _sigmoid(mm(4))


def _proj0(x, g, sc, sh, w):
    B, L, D = x.shape
    wd = SB_HEADS * HEAD_DIM
    tm = _row_tile(L, 512)
    row = lambda d, dt: (pl.BlockSpec((None, tm, d), lambda b, i: (b, i, 0)), jax.ShapeDtypeStruct((B, L, d), dt))
    outs = [row(wd, BF16), row(wd, F32), row(wd, F32), row(wd, BF16), row(wd, BF16), row(wd, F32)]
    return pl.pallas_call(
        _proj0_kernel,
        grid=(B, L // tm),
        in_specs=[pl.BlockSpec((None, tm, D), lambda b, i: (b, i, 0)),
                  pl.BlockSpec((1, D), lambda b, i: (0, 0)),
                  _mod_spec(sc, tm), _mod_spec(sh, tm),
                  pl.BlockSpec(w.shape, lambda b, i: (0, 0))],
        out_specs=[o[0] for o in outs],
        out_shape=[o[1] for o in outs],
        compiler_params=_cparams(("arbitrary", "arbitrary")),
        name="l0_in_proj",
    )(x, g.reshape(1, D), sc, sh, w)


def _sb_kernel(q_ref, k_ref, v_ref, o_ref, *, tq, tk, past, nkb_max):
    i = pl.program_id(2)
    qbase = past + i * tq
    nkb = jnp.minimum((qbase + tq - 2 + tk) // tk, nkb_max)
    lane = lax.broadcasted_iota(I32, (1, LANES), 1)
    half = [lane < HEAD_DIM, lane >= HEAD_DIM]
    q = q_ref[...]
    qm = [jnp.where(half[hh], q, jnp.zeros_like(q)) for hh in range(2)]
    rr = lax.broadcasted_iota(I32, (tk, tk), 0)
    cc = lax.broadcasted_iota(I32, (tk, tk), 1)
    U = jnp.where(rr > cc, 1.0, 0.0).astype(BF16)
    dcol = lax.broadcasted_iota(I32, (tq, tk), 1) - lax.broadcasted_iota(I32, (tq, tk), 0)

    def make_body(masked):
        def body(jj, carry):
            r0, r1, acc = carry
            j = nkb - 1 - jj
            k0 = pl.multiple_of(j * tk, tk)
            kb = k_ref[pl.ds(k0, tk), :]
            vb = v_ref[pl.ds(k0, tk), :]
            mask = dcol < (qbase - k0)
            rs = [r0, r1]
            for hh in range(2):
                z = _dot_nt(qm[hh], kb)
                sp = jnp.log(1.0 + jnp.exp(-jnp.abs(z)))
                lb = jnp.minimum(z, 0.0) - sp
                l1m = -jnp.maximum(z, 0.0) - sp
                if masked:
                    l1m = jnp.where(mask, l1m, 0.0)
                hi, lo = _split2(l1m)
                cs = _dot(hi, U) + _dot(lo, U)
                w = jnp.exp(lb + cs + rs[hh])
                if masked:
                    w = jnp.where(mask, w, 0.0)
                vm = jnp.where(half[hh], vb, jnp.zeros_like(vb))
                acc = acc + _dot(w.astype(BF16), vm)
                rs[hh] = rs[hh] + cs[:, 0:1] + l1m[:, 0:1]
            return rs[0], rs[1], acc
        return body

    n_full = jnp.minimum(qbase // tk, nkb)
    z1 = jnp.zeros((tq, 1), F32)
    carry = lax.fori_loop(0, nkb - n_full, make_body(True), (z1, z1, jnp.zeros((tq, LANES), F32)))
    _, _, acc = lax.fori_loop(nkb - n_full, nkb, make_body(False), carry)
    o_ref[...] = acc.astype(o_ref.dtype)


def _sb_attention(q, k, v, past, tq, tk):
    B, L, W = q.shape
    S = k.shape[1]
    hp = W // LANES
    kern = functools.partial(_sb_kernel, tq=tq, tk=tk, past=past, nkb_max=S // tk)
    return pl.pallas_call(
        kern,
        grid=(B, hp, L // tq),
        in_specs=[pl.BlockSpec((None, tq, LANES), lambda b, h, i: (b, i, h)),
                  pl.BlockSpec((None, S, LANES), lambda b, h, i: (b, 0, h)),
                  pl.BlockSpec((None, S, LANES), lambda b, h, i: (b, 0, h))],
        out_specs=pl.BlockSpec((None, tq, LANES), lambda b, h, i: (b, i, h)),
        out_shape=jax.ShapeDtypeStruct((B, L, W), BF16),
        compiler_params=_cparams(("arbitrary", "arbitrary", "arbitrary")),
        name="stickbreak_attn",
    )(q, k, v)


def _conv_kernel(*refs, tm, has_halo):
    if has_halo:
        u_ref, halo_ref, prev_ref, w_ref, b_ref, g_ref, be_ref, o_ref, buf_ref, cv_ref = refs
    else:
        u_ref, prev_ref, w_ref, b_ref, g_ref, be_ref, o_ref, buf_ref, cv_ref = refs
    i = pl.program_id(1)
    pad = 32
    if has_halo:
        buf_ref[0:pad, :] = jnp.where(i == 0, prev_ref[...], halo_ref[...])
    else:
        buf_ref[0:pad, :] = prev_ref[...]
    buf_ref[pad:pad + tm, :] = u_ref[...]
    C = u_ref.shape[-1]
    rt = min(tm, 64)
    off = pad - (CONV_WIDTH - 1)
    for c in range(C // LANES):
        cs = slice(c * LANES, (c + 1) * LANES)
        for r in range(tm // rt):
            acc = jnp.zeros((rt, LANES), F32)
            for j in range(CONV_WIDTH):
                acc = acc + buf_ref[r * rt + off + j: r * rt + off + j + rt, cs] * w_ref[j:j + 1, cs]
            cv_ref[r * rt:(r + 1) * rt, cs] = acc
    conv = cv_ref[...] + b_ref[...]
    mu = jnp.mean(conv, axis=-1, keepdims=True)
    xc = conv - mu
    var = jnp.mean(xc * xc, axis=-1, keepdims=True)
    y = xc * lax.rsqrt(var + EPS) * g_ref[...] + be_ref[...]
    o_ref[...] = (y * _sigmoid(y)).astype(o_ref.dtype)


def _conv_module(u, prev32, w, b, g, be):
    B, L, C = u.shape
    tm = _row_tile(L, 256)
    has_halo = L > tm
    kern = functools.partial(_conv_kernel, tm=tm, has_halo=has_halo)
    in_specs = [pl.BlockSpec((None, tm, C), lambda b_, i: (b_, i, 0))]
    args = [u]
    if has_halo:
        r = tm // 32
        in_specs.append(pl.BlockSpec((None, 32, C), lambda b_, i: (b_, jnp.maximum(i * r - 1, 0), 0)))
        args.append(u)
    vec = pl.BlockSpec((1, C), lambda b_, i: (0, 0))
    in_specs += [pl.BlockSpec((None, 32, C), lambda b_, i: (b_, 0, 0)),
                 pl.BlockSpec((32, C), lambda b_, i: (0, 0)), vec, vec, vec]
    wpad = jnp.concatenate([w, jnp.zeros((32 - CONV_WIDTH, C), F32)], axis=0)
    args += [prev32, wpad, b.reshape(1, C), g.reshape(1, C), be.reshape(1, C)]
    return pl.pallas_call(
        kern,
        grid=(B, L // tm),
        in_specs=in_specs,
        out_specs=pl.BlockSpec((None, tm, C), lambda b_, i: (b_, i, 0)),
        out_shape=jax.ShapeDtypeStruct((B, L, C), BF16),
        scratch_shapes=[pltpu.VMEM((tm + 32, C), F32), pltpu.VMEM((tm, C), F32)],
        compiler_params=_cparams(("arbitrary", "arbitrary")),
        name="conv_module",
    )(*args)


def _outproj_kernel(*refs, n_in):
    x_ref, g_ref = refs[0], refs[1]
    a_refs = refs[2:2 + n_in]
    w_refs = refs[2 + n_in:2 + 2 * n_in]
    o_ref = refs[2 + 2 * n_in]
    acc = _dot(a_refs[0][...], w_refs[0][...])
    for a_ref, w_ref in zip(a_refs[1:], w_refs[1:]):
        acc = acc + _dot(a_ref[...], w_ref[...])
    o_ref[...] = x_ref[...] + g_ref[...] * acc


def _outproj(x, gate, acts, ws):
    B, L, D = x.shape
    tm = _row_tile(L, 512)
    n_in = len(acts)
    in_specs = [pl.BlockSpec((None, tm, D), lambda b, i: (b, i, 0)), _mod_spec(gate, tm)]
    in_specs += [pl.BlockSpec((None, tm, a.shape[-1]), lambda b, i: (b, i, 0)) for a in acts]
    in_specs += [pl.BlockSpec(w.shape, lambda b, i: (0, 0)) for w in ws]
    return pl.pallas_call(
        functools.partial(_outproj_kernel, n_in=n_in),
        grid=(B, L // tm),
        in_specs=in_specs,
        out_specs=pl.BlockSpec((None, tm, D), lambda b, i: (b, i, 0)),
        out_shape=jax.ShapeDtypeStruct((B, L, D), F32),
        compiler_params=_cparams(("arbitrary", "arbitrary")),
        name="out_proj",
    )(x, gate, *acts, *ws)


def _ffn_kernel(x_ref, g_ref, sc_ref, sh_ref, gate_ref, wg_ref, wu_ref, wd_ref, o_ref, h_ref, acc_ref):
    f = pl.program_id(2)

    @pl.when(f == 0)
    def _():
        h_ref[...] = _modnorm(x_ref[...], g_ref[...], sc_ref[...], sh_ref[...]).astype(BF16)
        acc_ref[...] = jnp.zeros_like(acc_ref)

    h = h_ref[...]
    a = _dot(h, wg_ref[...])
    a = a * _sigmoid(a) * _dot(h, wu_ref[...])
    acc_ref[...] += _dot(a.astype(BF16), wd_ref[...])

    @pl.when(f == pl.num_programs(2) - 1)
    def _():
        o_ref[...] = x_ref[...] + gate_ref[...] * acc_ref[...]


def _ffn(x, g, sc, sh, gate, wg, wu, wd, tf):
    B, L, D = x.shape
    F = wg.shape[1]
    tm = _row_tile(L, 512)
    return pl.pallas_call(
        _ffn_kernel,
        grid=(B, L // tm, F // tf),
        in_specs=[pl.BlockSpec((None, tm, D), lambda b, i, f: (b, i, 0)),
                  pl.BlockSpec((1, D), lambda b, i, f: (0, 0)),
                  _mod_spec(sc, tm), _mod_spec(sh, tm), _mod_spec(gate, tm),
                  pl.BlockSpec((D, tf), lambda b, i, f: (0, f)),
                  pl.BlockSpec((D, tf), lambda b, i, f: (0, f)),
                  pl.BlockSpec((tf, D), lambda b, i, f: (f, 0))],
        out_specs=pl.BlockSpec((None, tm, D), lambda b, i, f: (b, i, 0)),
        out_shape=jax.ShapeDtypeStruct((B, L, D), F32),
        scratch_shapes=[pltpu.VMEM((tm, D), BF16), pltpu.VMEM((tm, D), F32)],
        compiler_params=_cparams(("arbitrary", "arbitrary", "arbitrary")),
        name="dense_swiglu",
    )(x, g.reshape(1, D), sc, sh, gate, wg, wu, wd)


def _dot_tn(a, b):
    return lax.dot_general(a, b, (((0,), (0,)), ((), ())), preferred_element_type=F32)


MOE_GROUP = 128
RANK_BLOCK = 256


def _moe_kernel(x_ref, g_ref, sc_ref, sh_ref, gate_ref, r_ref, wg_ref, wu_ref, wd_ref, o_ref,
                h_ref, rk_ref, rankT_ref, dgT_ref, hc_ref, yc_ref, *, tm):
    e = pl.program_id(2)
    f = pl.program_id(3)
    nf = pl.num_programs(3)
    gs = MOE_GROUP
    lane = lax.broadcasted_iota(I32, (1, LANES), 1)

    @pl.when((e == 0) & (f == 0))
    def _():
        h = _modnorm(x_ref[...], g_ref[...], sc_ref[...], sh_ref[...]).astype(BF16)
        h_ref[...] = h
        o_ref[...] = jnp.zeros_like(o_ref)
        logits = jnp.where(lane < N_EXPERTS, _dot(h, r_ref[...]), -jnp.inf)
        m1 = jnp.max(logits, axis=-1, keepdims=True)
        i1 = jnp.min(jnp.where(logits == m1, lane, LANES), axis=-1, keepdims=True)
        rest = jnp.where(lane == i1, -jnp.inf, logits)
        m2 = jnp.max(rest, axis=-1, keepdims=True)
        i2 = jnp.min(jnp.where(rest == m2, lane, LANES), axis=-1, keepdims=True)
        e2 = jnp.exp(m2 - m1)
        den = 1.0 + e2
        dg = jnp.where(lane == i1, 1.0 / den, 0.0) + jnp.where(lane == i2, e2 / den, 0.0)
        sel = jnp.where(lane == i1, 1.0, 0.0) + jnp.where(lane == i2, 1.0, 0.0)
        rb = min(RANK_BLOCK, tm)
        rr = lax.broadcasted_iota(I32, (rb, rb), 0)
        cc = lax.broadcasted_iota(I32, (rb, rb), 1)
        lower = jnp.where(cc < rr, 1.0, 0.0).astype(BF16)
        carry = jnp.zeros((1, LANES), F32)
        for b in range(tm // rb):
            sb = sel[b * rb:(b + 1) * rb]
            rk_ref[b * rb:(b + 1) * rb, :] = carry + _dot(lower, sb.astype(BF16))
            carry = carry + jnp.sum(sb, axis=0, keepdims=True)
        rank = jnp.where(sel > 0.5, rk_ref[...], -1.0)
        rankT_ref[...] = rank.T
        dgT_ref[...] = dg.T

    rk = rankT_ref[pl.ds(e, 1), :]
    gt = dgT_ref[pl.ds(e, 1), :]
    cnt = jnp.sum(jnp.where(rk >= 0.0, 1.0, 0.0), axis=1, keepdims=True).astype(I32)[0, 0]
    n_groups = (cnt + gs - 1) // gs
    row = lax.broadcasted_iota(I32, (gs, 1), 0)

    def group(gi, _):
        r0 = pl.multiple_of(gi * gs, gs)
        hit = rk == (row + r0).astype(F32)
        p = jnp.where(hit, 1.0, 0.0).astype(BF16)

        @pl.when(f == 0)
        def _():
            hc_ref[pl.ds(r0, gs), :] = _dot(p, h_ref[...]).astype(BF16)

        hg = hc_ref[pl.ds(r0, gs), :]
        a = _dot(hg, wg_ref[...])
        a = a * _sigmoid(a) * _dot(hg, wu_ref[...])
        y = _dot(a.astype(BF16), wd_ref[...])

        @pl.when(f == 0)
        def _():
            yc_ref[pl.ds(r0, gs), :] = y

        @pl.when(f != 0)
        def _():
            yc_ref[pl.ds(r0, gs), :] += y

        @pl.when(f == nf - 1)
        def _():
            gg = jnp.sum(jnp.where(hit, gt, 0.0), axis=1, keepdims=True)
            hi, lo = _split2(yc_ref[pl.ds(r0, gs), :] * gg)
            o_ref[...] += _dot_tn(jnp.concatenate([p, p], axis=0), jnp.concatenate([hi, lo], axis=0))

        return 0

    lax.fori_loop(0, n_groups, group, 0)

    @pl.when((e == pl.num_programs(2) - 1) & (f == nf - 1))
    def _():
        o_ref[...] = x_ref[...] + gate_ref[...] * o_ref[...]


def _moe(x, g, sc, sh, gate, router, wg, wu, wd, tf):
    B, L, D = x.shape
    E, _, F = wg.shape
    tm = _row_tile(L, 1024)
    assert tm % MOE_GROUP == 0
    return pl.pallas_call(
        functools.partial(_moe_kernel, tm=tm),
        grid=(B, L // tm, E, F // tf),
        in_specs=[pl.BlockSpec((None, tm, D), lambda b, i, e, f: (b, i, 0)),
                  pl.BlockSpec((1, D), lambda b, i, e, f: (0, 0)),
                  _mod_spec(sc, tm), _mod_spec(sh, tm), _mod_spec(gate, tm),
                  pl.BlockSpec(router.shape, lambda b, i, e, f: (0, 0)),
                  pl.BlockSpec((None, D, tf), lambda b, i, e, f: (e, 0, f)),
                  pl.BlockSpec((None, D, tf), lambda b, i, e, f: (e, 0, f)),
                  pl.BlockSpec((None, tf, D), lambda b, i, e, f: (e, f, 0))],
        out_specs=pl.BlockSpec((None, tm, D), lambda b, i, e, f: (b, i, 0)),
        out_shape=jax.ShapeDtypeStruct((B, L, D), F32),
        scratch_shapes=[pltpu.VMEM((tm, D), BF16), pltpu.VMEM((tm, LANES), F32),
                        pltpu.VMEM((LANES, tm), F32), pltpu.VMEM((LANES, tm), F32),
                        pltpu.VMEM((tm, D), BF16), pltpu.VMEM((tm, D), F32)],
        compiler_params=_cparams(("arbitrary",) * 4),
        name="expert_swiglu",
    )(x, g.reshape(1, D), sc, sh, gate, router, wg, wu, wd)


_P1_Q = 0
_P1_K = 1024
_P1_V = 1280
_P1_QI = 1536
_P1_KD = 2048
_P1_VD = 2560
_P1_KIWI = 3072
_P1_KID = 3200
_P1_N = 3328


def _rope(a, c, s1, s2):
    return a * c + pltpu.roll(a, LANES - ROPE_DIM // 2, 1) * s1 + pltpu.roll(a, ROPE_DIM // 2, 1) * s2


def _proj1_kernel(x_ref, g_ref, sc_ref, sh_ref, w_ref, wvt_ref, c_ref, s1_ref, s2_ref, qg_ref, kg_ref,
                  q_ref, k_ref, v_ref, qi_ref, kd_ref, vd_ref, kiwi_ref, kid_ref, *, vt):
    h = _modnorm(x_ref[...], g_ref[...], sc_ref[...], sh_ref[...]).astype(BF16)
    cos, s1, s2 = c_ref[...], s1_ref[...], s2_ref[...]
    rr = lax.broadcasted_iota(I32, (LANES, LANES), 0) // HEAD_DIM
    cc = lax.broadcasted_iota(I32, (LANES, LANES), 1) // HEAD_DIM
    bd = jnp.where(rr == cc, 1.0, 0.0).astype(BF16)
    lane = lax.broadcasted_iota(I32, (1, LANES), 1)

    def headnorm(a, gain):
        hi, lo = _split2(a * a)
        ss = _dot(hi, bd) + _dot(lo, bd)
        return a * lax.rsqrt(ss * (1.0 / HEAD_DIM) + EPS) * gain

    def group(c0, width):
        y = _dot(h, w_ref[:, c0:c0 + width])
        return [y[:, i * LANES:(i + 1) * LANES] for i in range(width // LANES)]

    qscale = HEAD_DIM ** -0.5
    for gidx in range(2):
        for i, a in enumerate(group(_P1_Q + gidx * 512, 512)):
            a = _rope(headnorm(a, qg_ref[...]), cos, s1, s2)
            cidx = gidx * 4 + i
            q_ref[:, cidx * LANES:(cidx + 1) * LANES] = (a * qscale).astype(BF16)
    kv = group(_P1_K, 512)
    for i in range(2):
        k_ref[:, i * LANES:(i + 1) * LANES] = _rope(headnorm(kv[i], kg_ref[...]), cos, s1, s2)
        v_ref[:, i * LANES:(i + 1) * LANES] = kv[2 + i]
    for i, a in enumerate(group(_P1_QI, 512)):
        qi_ref[:, i * LANES:(i + 1) * LANES] = (_rope(a, cos, s1, s2) * (HEAD_DIM ** -0.5)).astype(BF16)
    for i, a in enumerate(group(_P1_KD, 512)):
        kd_ref[:, i * LANES:(i + 1) * LANES] = _rope(headnorm(a, kg_ref[...]), cos, s1, s2).astype(BF16)
    if vt:
        vd_ref[...] = _dot_nt(wvt_ref[...], h).astype(BF16)
    else:
        for i, a in enumerate(group(_P1_VD, 512)):
            vd_ref[:, i * LANES:(i + 1) * LANES] = a.astype(BF16)
    kiwi, kid = group(_P1_KIWI, 256)
    first = lane < HEAD_DIM
    kiwi = _rope(kiwi, jnp.where(first, cos, 1.0), jnp.where(first, s1, 0.0), jnp.where(first, s2, 0.0))
    is_wi = (lane >= HEAD_DIM) & (lane < HEAD_DIM + IDX_HEADS)
    kiwi_ref[...] = kiwi * jnp.where(is_wi, IDX_HEADS ** -0.5, 1.0)
    kid_ref[...] = _rope(kid, cos, s1, s2).astype(BF16)


def _proj1(x, g, sc, sh, w, rope_tabs, qg, kg, vt):
    B, L, D = x.shape
    tm = _row_tile(L, 512)
    row = lambda d, dt: (pl.BlockSpec((None, tm, d), lambda b, i: (b, i, 0)), jax.ShapeDtypeStruct((B, L, d), dt))
    vd_out = row(512, BF16)
    if vt:
        vd_out = (pl.BlockSpec((None, 512, tm), lambda b, i: (b, 0, i)), jax.ShapeDtypeStruct((B, 512, L), BF16))
    outs = [row(1024, BF16), row(256, F32), row(256, F32), row(512, BF16), row(512, BF16), vd_out,
            row(LANES, F32), row(LANES, BF16)]
    wvt = w[:, _P1_VD:_P1_VD + 512].T
    tab = pl.BlockSpec((tm, LANES), lambda b, i: (i, 0))
    vec = pl.BlockSpec((1, LANES), lambda b, i: (0, 0))
    return pl.pallas_call(
        functools.partial(_proj1_kernel, vt=vt),
        grid=(B, L // tm),
        in_specs=[pl.BlockSpec((None, tm, D), lambda b, i: (b, i, 0)),
                  pl.BlockSpec((1, D), lambda b, i: (0, 0)),
                  _mod_spec(sc, tm), _mod_spec(sh, tm),
                  pl.BlockSpec(w.shape, lambda b, i: (0, 0)),
                  pl.BlockSpec(wvt.shape, lambda b, i: (0, 0)),
                  tab, tab, tab, vec, vec],
        out_specs=[o[0] for o in outs],
        out_shape=[o[1] for o in outs],
        compiler_params=_cparams(("arbitrary", "arbitrary")),
        name="l1_in_proj",
    )(x, g.reshape(1, D), sc, sh, w, wvt, *rope_tabs, qg, kg)


def _pad_l1_weight(w):
    D = w.shape[0]
    q, k, v, qi, ki, wi = jnp.split(w, [1024, 1280, 1536, 2048, 2112], axis=1)
    dup = lambda m, nh: jnp.repeat(m.reshape(D, nh, 1, HEAD_DIM), 2, axis=2).reshape(D, nh * 2 * HEAD_DIM)
    kiwi = jnp.concatenate([ki, wi, jnp.zeros((D, LANES - HEAD_DIM - IDX_HEADS), w.dtype)], axis=1)
    out = jnp.concatenate([q, k, v, qi, dup(k, DSA_KV_HEADS), dup(v, DSA_KV_HEADS), kiwi, dup(ki, 1)], axis=1)
    assert out.shape[1] == _P1_N
    return out.astype(BF16)


def _rope_tables(pos):
    half = ROPE_DIM // 2
    inv = 1.0 / (ROPE_THETA ** (jnp.arange(half, dtype=F32) / half))
    ang = pos.astype(F32)[:, None] * inv[None, :]
    cos, sin = jnp.cos(ang), jnp.sin(ang)
    n = pos.shape[0]
    one = jnp.ones((n, HEAD_DIM - ROPE_DIM), F32)
    zero = jnp.zeros((n, HEAD_DIM - ROPE_DIM), F32)
    z8 = jnp.zeros((n, half), F32)
    c = jnp.concatenate([cos, cos, one], axis=1)
    s1 = jnp.concatenate([-sin, z8, zero], axis=1)
    s2 = jnp.concatenate([z8, sin, zero], axis=1)
    return tuple(jnp.tile(t, (1, 2)) for t in (c, s1, s2))


def _sort_key(score):
    b = lax.bitcast_convert_type(score + 0.0, I32)
    return jnp.where(b < 0, b ^ 0x7FFFFFFF, b)


def _sort_key_const(value):
    b = int(np.float32(value).view(np.int32))
    return b ^ 0x7FFFFFFF if b < 0 else b


def _topk_kernel(qi_ref, kiwi_ref, kid_ref, mask_ref, key_ref, *, tq, tk, past, s_valid, topk, nkb_max, mask_t):
    i = pl.program_id(1)
    qbase = past + i * tq
    qrow = qbase + lax.broadcasted_iota(I32, (tq, 1), 0)
    adm_lim = jnp.minimum(((qrow >> 6) + 1) << 6, s_valid)
    lim = jnp.minimum((((qbase + tq - 1) >> 6) + 1) << 6, s_valid)
    nkb = jnp.minimum((lim + tk - 1) // tk, nkb_max)
    n_out = (jnp.zeros((tq, 1), I32) + (s_valid - jnp.minimum(nkb * tk, s_valid))).astype(F32)
    lane = lax.broadcasted_iota(I32, (1, LANES), 1)
    half = [lane < HEAD_DIM, lane >= HEAD_DIM]
    col = lax.broadcasted_iota(I32, (tq, tk), 1)
    negkey = _sort_key_const(NEG)
    kiwi = kiwi_ref[...]
    wi = [kiwi[:, HEAD_DIM + h:HEAD_DIM + h + 1] for h in range(IDX_HEADS)]
    qi = qi_ref[...]
    qim = []
    for h in range(IDX_HEADS):
        qc = qi[:, (h // 2) * LANES:(h // 2 + 1) * LANES]
        qim.append(jnp.where(half[h % 2], qc, jnp.zeros_like(qc)))

    def score_body(j, _):
        k0 = pl.multiple_of(j * tk, tk)
        kb = kid_ref[pl.ds(k0, tk), :]
        sc = jnp.zeros((tq, tk), F32)
        for h in range(IDX_HEADS):
            sc = sc + wi[h] * jnp.maximum(_dot_nt(qim[h], kb), 0.0)
        kpos = col + k0
        key = _sort_key(jnp.where(kpos < adm_lim, sc, NEG))
        key_ref[j] = jnp.where(kpos < s_valid, key, INT_MIN)
        return 0

    lax.fori_loop(0, nkb, score_body, 0)

    def count(thr, strict):
        def cbody(j, acc):
            kk = key_ref[j]
            for c in range(tk // LANES):
                kc = kk[:, c * LANES:(c + 1) * LANES]
                hit = (kc > thr) if strict else (kc >= thr)
                acc = acc + jnp.where(hit, 1.0, 0.0)
            return acc
        acc = lax.fori_loop(0, nkb, cbody, jnp.zeros((tq, LANES), F32))
        cnt = jnp.sum(acc, axis=-1, keepdims=True)
        out_hit = (negkey > thr) if strict else (negkey >= thr)
        return cnt + jnp.where(out_hit, n_out, 0.0)

    kf = float(topk)

    def bis_body(p, prefix):
        cand = prefix | jnp.left_shift(jnp.int32(1), 31 - p)
        cnt = count(cand ^ INT_MIN, False)
        return jnp.where(cnt >= kf, cand, prefix)

    prefix = lax.fori_loop(0, 32, bis_body, jnp.zeros((tq, 1), I32))
    tau = prefix ^ INT_MIN
    need = kf - count(tau, True)

    rr = lax.broadcasted_iota(I32, (tk, tk), 0)
    cc = lax.broadcasted_iota(I32, (tk, tk), 1)
    U = jnp.where(rr < cc, 1.0, 0.0).astype(BF16)

    def sel_body(j, carry):
        kk = key_ref[j]
        eq = jnp.where(kk == tau, 1.0, 0.0)
        rank = carry + _dot(eq.astype(BF16), U)
        take = jnp.where(kk > tau, 1.0, jnp.where(rank < need, eq, 0.0))
        kpos = col + j * tk
        mk = jnp.where(kpos < adm_lim, take, 0.0)
        mask_ref[j] = (mk.T if mask_t else mk).astype(mask_ref.dtype)
        return carry + jnp.sum(eq, axis=-1, keepdims=True)

    lax.fori_loop(0, nkb, sel_body, jnp.zeros((tq, 1), F32))

    def zero_body(j, _):
        mask_ref[j] = jnp.zeros(mask_ref.shape[1:], mask_ref.dtype)
        return 0

    lax.fori_loop(nkb, nkb_max, zero_body, 0)


def _topk_mask(qi, kiwi, kid, past, s_valid, tq, tk, mask_t):
    B, L, _ = qi.shape
    S = kid.shape[1]
    nb = S // tk
    topk = min(TOPK_MAX, s_valid // 4)
    kern = functools.partial(_topk_kernel, tq=tq, tk=tk, past=past, s_valid=s_valid, topk=topk, nkb_max=nb,
                             mask_t=mask_t)
    mshape = (tk, tq) if mask_t else (tq, tk)
    return pl.pallas_call(
        kern,
        grid=(B, L // tq),
        in_specs=[pl.BlockSpec((None, tq, qi.shape[-1]), lambda b, i: (b, i, 0)),
                  pl.BlockSpec((None, tq, LANES), lambda b, i: (b, i, 0)),
                  pl.BlockSpec((None, S, LANES), lambda b, i: (b, 0, 0))],
        out_specs=pl.BlockSpec((None, None, nb) + mshape, lambda b, i: (b, i, 0, 0, 0)),
        out_shape=jax.ShapeDtypeStruct((B, L // tq, nb) + mshape, BF16),
        scratch_shapes=[pltpu.VMEM((nb, tq, tk), I32)],
        compiler_params=_cparams(("arbitrary", "arbitrary")),
        name="indexer_topk",
    )(qi, kiwi, kid)


DSA_CHAINS = 2


def _dsa_kernel(q_ref, kd_ref, vd_ref, mask_ref, o_ref, m_ref, l_ref, acc_ref, *, tq, tk, past, s_valid, nkb_max):
    i = pl.program_id(1)
    qbase = past + i * tq
    lim = jnp.minimum((((qbase + tq - 1) >> 6) + 1) << 6, s_valid)
    nkb = jnp.minimum((lim + tk - 1) // tk, nkb_max)
    lane = lax.broadcasted_iota(I32, (1, LANES), 1)
    first = lane < HEAD_DIM
    hpp = m_ref.shape[0]
    for j0 in range(0, DSA_KV_HEADS, hpp):
        q4s = []
        for j in range(j0, j0 + hpp):
            qs = []
            for c in (2 * j, 2 * j + 1):
                qc = q_ref[:, c * LANES:(c + 1) * LANES]
                qs += [jnp.where(first, qc, jnp.zeros_like(qc)), jnp.where(first, jnp.zeros_like(qc), qc)]
            q4s.append(jnp.concatenate(qs, axis=0))
        m_ref[...] = jnp.full(m_ref.shape, NEG, F32)
        l_ref[...] = jnp.zeros_like(l_ref)
        acc_ref[...] = jnp.zeros_like(acc_ref)

        def body(jb, _):
            k0 = pl.multiple_of(jb * tk, tk)
            mk = mask_ref[jb].astype(F32)
            sel4 = jnp.concatenate([mk] * 4, axis=0) > 0.5
            for jj in range(hpp):
                j = j0 + jj
                kb = kd_ref[pl.ds(k0, tk), j * LANES:(j + 1) * LANES]
                vb = vd_ref[pl.ds(k0, tk), j * LANES:(j + 1) * LANES]
                s = jnp.where(sel4, _dot_nt(q4s[jj], kb), -jnp.inf)
                m_old = m_ref[jj]
                m_new = jnp.maximum(m_old, jnp.max(s, axis=-1, keepdims=True))
                alpha = jnp.exp(m_old - m_new)
                p = jnp.exp(s - m_new)
                l_ref[jj] = alpha * l_ref[jj] + jnp.sum(p, axis=-1, keepdims=True)
                acc_ref[jj] = alpha * acc_ref[jj] + _dot(p.astype(BF16), vb)
                m_ref[jj] = m_new
            return 0

        lax.fori_loop(0, nkb, body, 0)
        for jj in range(hpp):
            o4 = acc_ref[jj] / l_ref[jj]
            for cc in range(2):
                oc = jnp.where(first, o4[(2 * cc) * tq:(2 * cc + 1) * tq], o4[(2 * cc + 1) * tq:(2 * cc + 2) * tq])
                c = 2 * (j0 + jj) + cc
                o_ref[:, c * LANES:(c + 1) * LANES] = oc.astype(o_ref.dtype)


def _dsa_t_kernel(q_ref, kd_ref, vt_ref, mask_ref, o_ref, acc_ref, *, tq, tk, past, s_valid, nkb_max):
    i = pl.program_id(1)
    qbase = past + i * tq
    lim = jnp.minimum((((qbase + tq - 1) >> 6) + 1) << 6, s_valid)
    nkb = jnp.minimum((lim + tk - 1) // tk, nkb_max)
    lane = lax.broadcasted_iota(I32, (1, LANES), 1)
    first = lane < HEAD_DIM
    nch = acc_ref.shape[0]
    for j0 in range(0, DSA_KV_HEADS, nch):
        q4s = []
        for j in range(j0, j0 + nch):
            qs = []
            for c in (2 * j, 2 * j + 1):
                qc = q_ref[:, c * LANES:(c + 1) * LANES]
                qs += [jnp.where(first, qc, jnp.zeros_like(qc)), jnp.where(first, jnp.zeros_like(qc), qc)]
            q4s.append(jnp.concatenate(qs, axis=0))
        acc_ref[...] = jnp.zeros_like(acc_ref)

        def body(jb, carry):
            k0 = pl.multiple_of(jb * tk, tk)
            nsub = tk // mask_ref.shape[1]
            mk = jnp.concatenate([mask_ref[jb * nsub + t] for t in range(nsub)], axis=0).astype(F32)
            sel4 = jnp.concatenate([mk] * 4, axis=1) > 0.5
            out = []
            for jj in range(nch):
                j = j0 + jj
                m_old, l_old = carry[2 * jj], carry[2 * jj + 1]
                kb = kd_ref[pl.ds(k0, tk), j * LANES:(j + 1) * LANES]
                vb = vt_ref[j * LANES:(j + 1) * LANES, pl.ds(k0, tk)]
                s = jnp.where(sel4, _dot_nt(kb, q4s[jj]), -jnp.inf)
                m_new = jnp.maximum(m_old, jnp.max(s, axis=0, keepdims=True))
                alpha = jnp.exp(m_old - m_new)
                p = jnp.exp(s - m_new)
                l_new = alpha * l_old + jnp.sum(p, axis=0, keepdims=True)
                acc_ref[jj] = alpha * acc_ref[jj] + _dot(vb, p.astype(BF16))
                out += [m_new, l_new]
            return tuple(out)

        init = (jnp.full((1, 4 * tq), NEG, F32), jnp.zeros((1, 4 * tq), F32)) * nch
        stats = lax.fori_loop(0, nkb, body, init)
        for jj in range(nch):
            o4 = acc_ref[jj] / stats[2 * jj + 1]
            heads = [o4[:, g * tq:(g + 1) * tq].T for g in range(4)]
            for cc in range(2):
                c = 2 * (j0 + jj) + cc
                o_ref[:, c * LANES:(c + 1) * LANES] = jnp.where(first, heads[2 * cc], heads[2 * cc + 1]).astype(o_ref.dtype)


def _dsa_attention_t(q, kd, vt, mask_t, past, s_valid, tq, tk):
    B, L, W = q.shape
    S = kd.shape[1]
    nb = S // tk
    assert tq == LANES and tk % mask_t.shape[3] == 0 and S % tk == 0
    kern = functools.partial(_dsa_t_kernel, tq=tq, tk=tk, past=past, s_valid=s_valid, nkb_max=nb)
    return pl.pallas_call(
        kern,
        grid=(B, L // tq),
        in_specs=[pl.BlockSpec((None, tq, W), lambda b, i: (b, i, 0)),
                  pl.BlockSpec((None, S, kd.shape[-1]), lambda b, i: (b, 0, 0)),
                  pl.BlockSpec((None, vt.shape[1], S), lambda b, i: (b, 0, 0)),
                  pl.BlockSpec((None, None) + mask_t.shape[2:], lambda b, i: (b, i, 0, 0, 0))],
        out_specs=pl.BlockSpec((None, tq, W), lambda b, i: (b, i, 0)),
        out_shape=jax.ShapeDtypeStruct((B, L, W), BF16),
        scratch_shapes=[pltpu.VMEM((DSA_CHAINS, LANES, 4 * tq), F32)],
        compiler_params=_cparams(("arbitrary", "arbitrary")),
        name="sparse_attn_t",
    )(q, kd, vt, mask_t)


def _dsa_attention(q, kd, vd, mask, past, s_valid, tq, tk):
    B, L, W = q.shape
    S = kd.shape[1]
    nb = S // tk
    kern = functools.partial(_dsa_kernel, tq=tq, tk=tk, past=past, s_valid=s_valid, nkb_max=nb)
    return pl.pallas_call(
        kern,
        grid=(B, L // tq),
        in_specs=[pl.BlockSpec((None, tq, W), lambda b, i: (b, i, 0)),
                  pl.BlockSpec((None, S, kd.shape[-1]), lambda b, i: (b, 0, 0)),
                  pl.BlockSpec((None, S, vd.shape[-1]), lambda b, i: (b, 0, 0)),
                  pl.BlockSpec((None, None, nb, tq, tk), lambda b, i: (b, i, 0, 0, 0))],
        out_specs=pl.BlockSpec((None, tq, W), lambda b, i: (b, i, 0)),
        out_shape=jax.ShapeDtypeStruct((B, L, W), BF16),
        scratch_shapes=[pltpu.VMEM((DSA_CHAINS, 4 * tq, 1), F32), pltpu.VMEM((DSA_CHAINS, 4 * tq, 1), F32),
                        pltpu.VMEM((DSA_CHAINS, 4 * tq, LANES), F32)],
        compiler_params=_cparams(("arbitrary", "arbitrary")),
        name="sparse_attn",
    )(q, kd, vd, mask)


def _pad_rows(a, mult):
    s = a.shape[1]
    sp = -(-s // mult) * mult
    if sp == s:
        return a
    return jnp.concatenate([a, jnp.zeros((a.shape[0], sp - s) + a.shape[2:], a.dtype)], axis=1)


def _dup_heads(a):
    B, S, H, d = a.shape
    return jnp.repeat(a[:, :, :, None, :], 2, axis=3).reshape(B, S, H * 2 * d)


SB_TQ, SB_TK = 1024, 256
SB_TK_DECODE = 512
IDX_TK = 512
DSA_TK = 1024


def _trunk(x, mods0, mods1, past, W, flat):
    B, L, D = x.shape
    past_len = 0 if past is None else past[0].shape[1]

    def tok(a):
        return a.reshape(1, B * L, a.shape[-1]) if flat else a

    def untok(a):
        return a.reshape(B, L, a.shape[-1]) if flat else a

    def mod(m):
        if flat:
            return jnp.repeat(m, L, axis=0).reshape(1, B * L, D)
        return m[:, None, :]

    sh_m, sc_m, g_m, sh_f, sc_f, g_f = [mod(m) for m in mods0]
    xt = tok(x)

    q, k, v, kb, vb, u = _proj0(xt, W['l0_norm_mix'], sc_m, sh_m, W['l0_w_in'])
    q, k, v, kb, vb, u = [untok(a) for a in (q, k, v, kb, vb, u)]
    if past is None:
        tq, tk = _row_tile(L, SB_TQ), min(L, SB_TK)
        k_all, v_all = kb, vb
        prev = jnp.zeros((B, 32, u.shape[-1]), F32)
    else:
        tq, tk = L, SB_TK_DECODE
        k_all = _pad_rows(jnp.concatenate([past[0].reshape(B, past_len, -1).astype(BF16), kb], axis=1), tk)
        v_all = _pad_rows(jnp.concatenate([past[1].reshape(B, past_len, -1).astype(BF16), vb], axis=1), tk)
        prev = jnp.concatenate([jnp.zeros((B, 2, u.shape[-1]), F32), past[2]], axis=1)
    o_a = _sb_attention(q, k_all, v_all, past_len, tq, tk)
    o_b = _conv_module(u, prev, W['l0_conv_w'], W['l0_conv_b'], W['l0_conv_ln_g'], W['l0_conv_ln_b'])
    conv_state = jnp.concatenate([prev[:, 2:], u], axis=1)[:, -(CONV_WIDTH - 1):]
    xt = _outproj(xt, g_m, [tok(o_a), tok(o_b)], [W['l0_w_out'][:512], W['l0_w_out'][512:]])
    xt = _ffn(xt, W['l0_norm_ff'], sc_f, sh_f, g_f, W['l0_ff_wg'], W['l0_ff_wu'], W['l0_ff_wd'], 1408)

    sh_m, sc_m, g_m, sh_f, sc_f, g_f = [mod(m) for m in mods1]
    pos = past_len + jnp.arange(L, dtype=I32)
    tabs = _rope_tables(pos)
    if flat:
        tabs = tuple(jnp.tile(t, (B, 1)) for t in tabs)
    qg = jnp.tile(W['l1_q_norm'].reshape(1, HEAD_DIM), (1, 2))
    kg = jnp.tile(W['l1_k_norm'].reshape(1, HEAD_DIM), (1, 2))
    keys_on_sublanes = past is None and L % LANES == 0
    q1, k1, v1, qi, kd, vd, kiwi, kid = _proj1(xt, W['l1_norm_mix'], sc_m, sh_m, W['l1_w_in'], tabs, qg, kg,
                                               vt=keys_on_sublanes)
    q1, k1, v1, qi, kd, kiwi, kid = [untok(a) for a in (q1, k1, v1, qi, kd, kiwi, kid)]
    if not keys_on_sublanes:
        vd = untok(vd)
    s_valid = past_len + L
    tq1 = min(L, LANES)
    tk1 = min(L, IDX_TK) if past is None else IDX_TK
    if past is not None:
        kd = _pad_rows(jnp.concatenate([_dup_heads(past[3]).astype(BF16), kd], axis=1), tk1)
        vd = _pad_rows(jnp.concatenate([_dup_heads(past[4]).astype(BF16), vd], axis=1), tk1)
        kid = _pad_rows(jnp.concatenate([_dup_heads(past[5][:, :, None, :]).astype(BF16), kid], axis=1), tk1)
    mask = _topk_mask(qi, kiwi, kid, past_len, s_valid, tq1, tk1, mask_t=keys_on_sublanes)
    if keys_on_sublanes:
        tk_attn = DSA_TK if kd.shape[1] % DSA_TK == 0 else tk1
        o1 = _dsa_attention_t(q1, kd, vd, mask, past_len, s_valid, tq1, tk_attn)
    else:
        o1 = _dsa_attention(q1, kd, vd, mask, past_len, s_valid, tq1, tk1)
    xt = _outproj(xt, g_m, [tok(o1)], [W['l1_w_out']])
    xt = _moe(xt, W['l1_norm_ff'], sc_f, sh_f, g_f, W['l1_router'], W['l1_exp_wg'], W['l1_exp_wu'],
              W['l1_exp_wd'], 1792)

    hd = HEAD_DIM
    states = (k.reshape(B, L, SB_HEADS, hd), v.reshape(B, L, SB_HEADS, hd), conv_state,
              k1.reshape(B, L, DSA_KV_HEADS, hd), v1.reshape(B, L, DSA_KV_HEADS, hd), kiwi[..., :hd])
    return untok(xt), states


def kernel(x_prompt, x_sample, c_prompt, c_sample, cache_sb_k, cache_sb_v, cache_conv, cache_dsa_k, cache_dsa_v, cache_dsa_kidx, l0_ada_w, l0_ada_b, l0_norm_mix, l0_w_in, l0_conv_w, l0_conv_b, l0_conv_ln_g, l0_conv_ln_b, l0_w_out, l0_norm_ff, l0_ff_wg, l0_ff_wu, l0_ff_wd, l1_ada_w, l1_ada_b, l1_norm_mix, l1_w_in, l1_q_norm, l1_k_norm, l1_w_out, l1_norm_ff, l1_router, l1_exp_wg, l1_exp_wu, l1_exp_wd):
    D = x_prompt.shape[-1]
    router = jnp.concatenate([l1_router, jnp.zeros((D, LANES - N_EXPERTS), F32)], axis=1).astype(BF16)
    W = dict(l0_norm_mix=l0_norm_mix, l0_w_in=l0_w_in.astype(BF16), l0_conv_w=l0_conv_w, l0_conv_b=l0_conv_b,
             l0_conv_ln_g=l0_conv_ln_g, l0_conv_ln_b=l0_conv_ln_b, l0_w_out=l0_w_out.astype(BF16),
             l0_norm_ff=l0_norm_ff, l0_ff_wg=l0_ff_wg.astype(BF16), l0_ff_wu=l0_ff_wu.astype(BF16),
             l0_ff_wd=l0_ff_wd.astype(BF16), l1_norm_mix=l1_norm_mix, l1_w_in=_pad_l1_weight(l1_w_in),
             l1_q_norm=l1_q_norm, l1_k_norm=l1_k_norm, l1_w_out=l1_w_out.astype(BF16), l1_norm_ff=l1_norm_ff,
             l1_router=router, l1_exp_wg=l1_exp_wg.astype(BF16), l1_exp_wu=l1_exp_wu.astype(BF16),
             l1_exp_wd=l1_exp_wd.astype(BF16))
    bp = c_prompt.shape[0]
    c_all = jnp.concatenate([c_prompt, c_sample], axis=0)
    m0 = _ada(c_all, l0_ada_w, l0_ada_b)
    m1 = _ada(c_all, l1_ada_w, l1_ada_b)
    mods = lambda m, sl: [t[sl] for t in jnp.split(m, 6, axis=-1)]
    y_p, st_p = _trunk(x_prompt, mods(m0, slice(0, bp)), mods(m1, slice(0, bp)), None, W, flat=False)
    past = (cache_sb_k, cache_sb_v, cache_conv, cache_dsa_k, cache_dsa_v, cache_dsa_kidx)
    y_s, st_s = _trunk(x_sample, mods(m0, slice(bp, None)), mods(m1, slice(bp, None)), past, W, flat=True)
    return (y_p, y_s) + tuple(st_p) + tuple(st_s)
```

```python
import functools
import math

import numpy as np
import jax
import jax.numpy as jnp
from jax import lax
from jax.experimental import pallas as pl
from jax.experimental.pallas import tpu as pltpu

F32 = jnp.float32
BF16 = jnp.bfloat16
I32 = jnp.int32

EPS = 1e-6
NEG = -1e30
HEAD_DIM = 64
CHUNK = 64
TOPK_MAX = 256
ROPE_DIM = 16
ROPE_THETA = 500000.0
CONV_WIDTH = 31
SB_HEADS = 8
DSA_HEADS = 16
DSA_KV_HEADS = 4
IDX_HEADS = 8
N_EXPERTS = 8
LANES = 128
INT_MIN = -2 ** 31
VMEM_LIMIT = 56 * 2 ** 20


def _cparams(sem):
    return pltpu.CompilerParams(dimension_semantics=sem, vmem_limit_bytes=VMEM_LIMIT)


def _dot(a, b):
    return jnp.dot(a, b, preferred_element_type=F32)


def _dot_nt(a, b):
    return lax.dot_general(a, b, (((1,), (1,)), ((), ())), preferred_element_type=F32)


def _split2(x):
    hi = x.astype(BF16)
    lo = (x - hi.astype(F32)).astype(BF16)
    return hi, lo


def _sigmoid(x):
    return 1.0 / (1.0 + jnp.exp(-x))


def _modnorm(x, g, sc, sh):
    ms = jnp.mean(x * x, axis=-1, keepdims=True)
    y = x * lax.rsqrt(ms + EPS) * g
    return y * (1.0 + sc) + sh


def _mod_spec(mod, tm):
    if mod.shape[1] == 1:
        return pl.BlockSpec((None, 1, mod.shape[2]), lambda b, i, *_: (b, 0, 0))
    return pl.BlockSpec((None, tm, mod.shape[2]), lambda b, i, *_: (b, i, 0))


def _row_tile(L, pref):
    return pref if L % pref == 0 else L


def _ada_kernel(c_ref, w_ref, b_ref, o_ref):
    c = c_ref[...]
    s = c * _sigmoid(c)
    s_hi, s_lo = _split2(s)
    w_hi, w_lo = _split2(w_ref[...])
    o_ref[...] = _dot(s_hi, w_hi) + _dot(s_lo, w_hi) + _dot(s_hi, w_lo) + b_ref[...]


def _ada(c, w, b):
    bc, d = c.shape
    n = w.shape[1]
    tn = 512
    return pl.pallas_call(
        _ada_kernel,
        grid=(n // tn,),
        in_specs=[pl.BlockSpec((bc, d), lambda j: (0, 0)),
                  pl.BlockSpec((d, tn), lambda j: (0, j)),
                  pl.BlockSpec((1, tn), lambda j: (0, j))],
        out_specs=pl.BlockSpec((bc, tn), lambda j: (0, j)),
        out_shape=jax.ShapeDtypeStruct((bc, n), F32),
        compiler_params=_cparams(("arbitrary",)),
        name="ada_mod",
    )(c, w, b.reshape(1, n))


def _proj0_kernel(x_ref, g_ref, sc_ref, sh_ref, w_ref, q_ref, k_ref, v_ref, kb_ref, vb_ref, u_ref):
    h = _modnorm(x_ref[...], g_ref[...], sc_ref[...], sh_ref[...]).astype(BF16)
    wd = q_ref.shape[-1]

    def mm(c):
        return _dot(h, w_ref[:, c * wd:(c + 1) * wd])

    q_ref[...] = (mm(0) * (HEAD_DIM ** -0.5)).astype(BF16)
    k = mm(1)
    k_ref[...] = k
    kb_ref[...] = k.astype(BF16)
    v = mm(2)
    v_ref[...] = v
    vb_ref[...] = v.astype(BF16)
    u_ref[...] = mm(3) * _sigmoid(mm(4))


def _proj0(x, g, sc, sh, w):
    B, L, D = x.shape
    wd = SB_HEADS * HEAD_DIM
    tm = _row_tile(L, 512)
    row = lambda d, dt: (pl.BlockSpec((None, tm, d), lambda b, i: (b, i, 0)), jax.ShapeDtypeStruct((B, L, d), dt))
    outs = [row(wd, BF16), row(wd, F32), row(wd, F32), row(wd, BF16), row(wd, BF16), row(wd, F32)]
    return pl.pallas_call(
        _proj0_kernel,
        grid=(B, L // tm),
        in_specs=[pl.BlockSpec((None, tm, D), lambda b, i: (b, i, 0)),
                  pl.BlockSpec((1, D), lambda b, i: (0, 0)),
                  _mod_spec(sc, tm), _mod_spec(sh, tm),
                  pl.BlockSpec(w.shape, lambda b, i: (0, 0))],
        out_specs=[o[0] for o in outs],
        out_shape=[o[1] for o in outs],
        compiler_params=_cparams(("arbitrary", "arbitrary")),
        name="l0_in_proj",
    )(x, g.reshape(1, D), sc, sh, w)


SB_EXP_UNDERFLOW = -120.0


def _sb_kernel(q_ref, k_ref, v_ref, o_ref, *, tq, tk, past, nkb_max):
    i = pl.program_id(2)
    qbase = past + i * tq
    nkb = jnp.minimum((qbase + tq - 2 + tk) // tk, nkb_max)
    lane = lax.broadcasted_iota(I32, (1, LANES), 1)
    half = [lane < HEAD_DIM, lane >= HEAD_DIM]
    q = q_ref[...]
    qm = [jnp.where(half[hh], q, jnp.zeros_like(q)) for hh in range(2)]
    rr = lax.broadcasted_iota(I32, (tk, tk), 0)
    cc = lax.broadcasted_iota(I32, (tk, tk), 1)
    U = jnp.where(rr > cc, 1.0, 0.0).astype(BF16)
    dcol = lax.broadcasted_iota(I32, (tq, tk), 1) - lax.broadcasted_iota(I32, (tq, tk), 0)

    def make_body(masked):
        def body(jj, carry):
            r0, r1, acc = carry
            j = nkb - 1 - jj
            k0 = pl.multiple_of(j * tk, tk)
            kb = k_ref[pl.ds(k0, tk), :]
            vb = v_ref[pl.ds(k0, tk), :]
            mask = dcol < (qbase - k0)
            rs = [r0, r1]
            for hh in range(2):
                z = _dot_nt(qm[hh], kb)
                sp = jnp.log(1.0 + jnp.exp(-jnp.abs(z)))
                lb = jnp.minimum(z, 0.0) - sp
                l1m = -jnp.maximum(z, 0.0) - sp
                if masked:
                    l1m = jnp.where(mask, l1m, 0.0)
                hi, lo = _split2(l1m)
                cs = _dot(hi, U) + _dot(lo, U)
                w = jnp.exp(lb + cs + rs[hh])
                if masked:
                    w = jnp.where(mask, w, 0.0)
                vm = jnp.where(half[hh], vb, jnp.zeros_like(vb))
                acc = acc + _dot(w.astype(BF16), vm)
                rs[hh] = rs[hh] + cs[:, 0:1] + l1m[:, 0:1]
            return rs[0], rs[1], acc
        return body

    n_full = jnp.minimum(qbase // tk, nkb)
    z1 = jnp.zeros((tq, 1), F32)
    carry = lax.fori_loop(0, nkb - n_full, make_body(True), (z1, z1, jnp.zeros((tq, LANES), F32)))
    unmasked = make_body(False)

    def live(r0, r1):
        return jnp.max(jnp.maximum(r0, r1)) > SB_EXP_UNDERFLOW

    def w_cond(c):
        return (c[0] < nkb) & c[1]

    def w_body(c):
        r0, r1, acc = unmasked(c[0], c[2:])
        return c[0] + 1, live(r0, r1), r0, r1, acc

    out = lax.while_loop(w_cond, w_body, (nkb - n_full, live(carry[0], carry[1])) + tuple(carry))
    o_ref[...] = out[4].astype(o_ref.dtype)


def _sb_attention(q, k, v, past, tq, tk):
    B, L, W = q.shape
    S = k.shape[1]
    hp = W // LANES
    kern = functools.partial(_sb_kernel, tq=tq, tk=tk, past=past, nkb_max=S // tk)
    return pl.pallas_call(
        kern,
        grid=(B, hp, L // tq),
        in_specs=[pl.BlockSpec((None, tq, LANES), lambda b, h, i: (b, i, h)),
                  pl.BlockSpec((None, S, LANES), lambda b, h, i: (b, 0, h)),
                  pl.BlockSpec((None, S, LANES), lambda b, h, i: (b, 0, h))],
        out_specs=pl.BlockSpec((None, tq, LANES), lambda b, h, i: (b, i, h)),
        out_shape=jax.ShapeDtypeStruct((B, L, W), BF16),
        compiler_params=_cparams(("arbitrary", "arbitrary", "arbitrary")),
        name="stickbreak_attn",
    )(q, k, v)


def _conv_kernel(*refs, tm, has_halo):
    if has_halo:
        u_ref, halo_ref, prev_ref, w_ref, b_ref, g_ref, be_ref, o_ref, buf_ref, cv_ref = refs
    else:
        u_ref, prev_ref, w_ref, b_ref, g_ref, be_ref, o_ref, buf_ref, cv_ref = refs
    i = pl.program_id(1)
    pad = 32
    if has_halo:
        buf_ref[0:pad, :] = jnp.where(i == 0, prev_ref[...], halo_ref[...])
    else:
        buf_ref[0:pad, :] = prev_ref[...]
    buf_ref[pad:pad + tm, :] = u_ref[...]
    C = u_ref.shape[-1]
    rt = min(tm, 64)
    off = pad - (CONV_WIDTH - 1)
    for c in range(C // LANES):
        cs = slice(c * LANES, (c + 1) * LANES)
        for r in range(tm // rt):
            acc = jnp.zeros((rt, LANES), F32)
            for j in range(CONV_WIDTH):
                acc = acc + buf_ref[r * rt + off + j: r * rt + off + j + rt, cs] * w_ref[j:j + 1, cs]
            cv_ref[r * rt:(r + 1) * rt, cs] = acc
    conv = cv_ref[...] + b_ref[...]
    mu = jnp.mean(conv, axis=-1, keepdims=True)
    xc = conv - mu
    var = jnp.mean(xc * xc, axis=-1, keepdims=True)
    y = xc * lax.rsqrt(var + EPS) * g_ref[...] + be_ref[...]
    o_ref[...] = (y * _sigmoid(y)).astype(o_ref.dtype)


def _conv_module(u, prev32, w, b, g, be):
    B, L, C = u.shape
    tm = _row_tile(L, 256)
    has_halo = L > tm
    kern = functools.partial(_conv_kernel, tm=tm, has_halo=has_halo)
    in_specs = [pl.BlockSpec((None, tm, C), lambda b_, i: (b_, i, 0))]
    args = [u]
    if has_halo:
        r = tm // 32
        in_specs.append(pl.BlockSpec((None, 32, C), lambda b_, i: (b_, jnp.maximum(i * r - 1, 0), 0)))
        args.append(u)
    vec = pl.BlockSpec((1, C), lambda b_, i: (0, 0))
    in_specs += [pl.BlockSpec((None, 32, C), lambda b_, i: (b_, 0, 0)),
                 pl.BlockSpec((32, C), lambda b_, i: (0, 0)), vec, vec, vec]
    wpad = jnp.concatenate([w, jnp.zeros((32 - CONV_WIDTH, C), F32)], axis=0)
    args += [prev32, wpad, b.reshape(1, C), g.reshape(1, C), be.reshape(1, C)]
    return pl.pallas_call(
        kern,
        grid=(B, L // tm),
        in_specs=in_specs,
        out_specs=pl.BlockSpec((None, tm, C), lambda b_, i: (b_, i, 0)),
        out_shape=jax.ShapeDtypeStruct((B, L, C), BF16),
        scratch_shapes=[pltpu.VMEM((tm + 32, C), F32), pltpu.VMEM((tm, C), F32)],
        compiler_params=_cparams(("arbitrary", "arbitrary")),
        name="conv_module",
    )(*args)


def _outproj_kernel(*refs, n_in):
    x_ref, g_ref = refs[0], refs[1]
    a_refs = refs[2:2 + n_in]
    w_refs = refs[2 + n_in:2 + 2 * n_in]
    o_ref = refs[2 + 2 * n_in]
    acc = _dot(a_refs[0][...], w_refs[0][...])
    for a_ref, w_ref in zip(a_refs[1:], w_refs[1:]):
        acc = acc + _dot(a_ref[...], w_ref[...])
    o_ref[...] = x_ref[...] + g_ref[...] * acc


def _outproj(x, gate, acts, ws):
    B, L, D = x.shape
    tm = _row_tile(L, 512)
    n_in = len(acts)
    in_specs = [pl.BlockSpec((None, tm, D), lambda b, i: (b, i, 0)), _mod_spec(gate, tm)]
    in_specs += [pl.BlockSpec((None, tm, a.shape[-1]), lambda b, i: (b, i, 0)) for a in acts]
    in_specs += [pl.BlockSpec(w.shape, lambda b, i: (0, 0)) for w in ws]
    return pl.pallas_call(
        functools.partial(_outproj_kernel, n_in=n_in),
        grid=(B, L // tm),
        in_specs=in_specs,
        out_specs=pl.BlockSpec((None, tm, D), lambda b, i: (b, i, 0)),
        out_shape=jax.ShapeDtypeStruct((B, L, D), F32),
        compiler_params=_cparams(("arbitrary", "arbitrary")),
        name="out_proj",
    )(x, gate, *acts, *ws)


def _ffn_kernel(x_ref, g_ref, sc_ref, sh_ref, gate_ref, wg_ref, wu_ref, wd_ref, o_ref, h_ref, acc_ref):
    f = pl.program_id(2)

    @pl.when(f == 0)
    def _():
        h_ref[...] = _modnorm(x_ref[...], g_ref[...], sc_ref[...], sh_ref[...]).astype(BF16)
        acc_ref[...] = jnp.zeros_like(acc_ref)

    h = h_ref[...]
    a = _dot(h, wg_ref[...])
    a = a * _sigmoid(a) * _dot(h, wu_ref[...])
    acc_ref[...] += _dot(a.astype(BF16), wd_ref[...])

    @pl.when(f == pl.num_programs(2) - 1)
    def _():
        o_ref[...] = x_ref[...] + gate_ref[...] * acc_ref[...]


def _ffn(x, g, sc, sh, gate, wg, wu, wd, tf):
    B, L, D = x.shape
    F = wg.shape[1]
    tm = _row_tile(L, 512)
    return pl.pallas_call(
        _ffn_kernel,
        grid=(B, L // tm, F // tf),
        in_specs=[pl.BlockSpec((None, tm, D), lambda b, i, f: (b, i, 0)),
                  pl.BlockSpec((1, D), lambda b, i, f: (0, 0)),
                  _mod_spec(sc, tm), _mod_spec(sh, tm), _mod_spec(gate, tm),
                  pl.BlockSpec((D, tf), lambda b, i, f: (0, f)),
                  pl.BlockSpec((D, tf), lambda b, i, f: (0, f)),
                  pl.BlockSpec((tf, D), lambda b, i, f: (f, 0))],
        out_specs=pl.BlockSpec((None, tm, D), lambda b, i, f: (b, i, 0)),
        out_shape=jax.ShapeDtypeStruct((B, L, D), F32),
        scratch_shapes=[pltpu.VMEM((tm, D), BF16), pltpu.VMEM((tm, D), F32)],
        compiler_params=_cparams(("arbitrary", "arbitrary", "arbitrary")),
        name="dense_swiglu",
    )(x, g.reshape(1, D), sc, sh, gate, wg, wu, wd)


def _dot_tn(a, b):
    return lax.dot_general(a, b, (((0,), (0,)), ((), ())), preferred_element_type=F32)


MOE_GROUP = 128
RANK_BLOCK = 256


def _moe_kernel(x_ref, g_ref, sc_ref, sh_ref, gate_ref, r_ref, wg_ref, wu_ref, wd_ref, o_ref,
                h_ref, rk_ref, rankT_ref, dgT_ref, hc_ref, yc_ref, *, tm):
    e = pl.program_id(2)
    f = pl.program_id(3)
    nf = pl.num_programs(3)
    gs = MOE_GROUP
    lane = lax.broadcasted_iota(I32, (1, LANES), 1)

    @pl.when((e == 0) & (f == 0))
    def _():
        h = _modnorm(x_ref[...], g_ref[...], sc_ref[...], sh_ref[...]).astype(BF16)
        h_ref[...] = h
        o_ref[...] = jnp.zeros_like(o_ref)
        logits = jnp.where(lane < N_EXPERTS, _dot(h, r_ref[...]), -jnp.inf)
        m1 = jnp.max(logits, axis=-1, keepdims=True)
        i1 = jnp.min(jnp.where(logits == m1, lane, LANES), axis=-1, keepdims=True)
        rest = jnp.where(lane == i1, -jnp.inf, logits)
        m2 = jnp.max(rest, axis=-1, keepdims=True)
        i2 = jnp.min(jnp.where(rest == m2, lane, LANES), axis=-1, keepdims=True)
        e2 = jnp.exp(m2 - m1)
        den = 1.0 + e2
        dg = jnp.where(lane == i1, 1.0 / den, 0.0) + jnp.where(lane == i2, e2 / den, 0.0)
        sel = jnp.where(lane == i1, 1.0, 0.0) + jnp.where(lane == i2, 1.0, 0.0)
        rb = min(RANK_BLOCK, tm)
        rr = lax.broadcasted_iota(I32, (rb, rb), 0)
        cc = lax.broadcasted_iota(I32, (rb, rb), 1)
        lower = jnp.where(cc < rr, 1.0, 0.0).astype(BF16)
        carry = jnp.zeros((1, LANES), F32)
        for b in range(tm // rb):
            sb = sel[b * rb:(b + 1) * rb]
            rk_ref[b * rb:(b + 1) * rb, :] = carry + _dot(lower, sb.astype(BF16))
            carry = carry + jnp.sum(sb, axis=0, keepdims=True)
        rank = jnp.where(sel > 0.5, rk_ref[...], -1.0)
        rankT_ref[...] = rank.T
        dgT_ref[...] = dg.T

    rk = rankT_ref[pl.ds(e, 1), :]
    gt = dgT_ref[pl.ds(e, 1), :]
    cnt = jnp.sum(jnp.where(rk >= 0.0, 1.0, 0.0), axis=1, keepdims=True).astype(I32)[0, 0]
    n_groups = (cnt + gs - 1) // gs
    row = lax.broadcasted_iota(I32, (gs, 1), 0)

    def group(gi, _):
        r0 = pl.multiple_of(gi * gs, gs)
        hit = rk == (row + r0).astype(F32)
        p = jnp.where(hit, 1.0, 0.0).astype(BF16)

        @pl.when(f == 0)
        def _():
            hc_ref[pl.ds(r0, gs), :] = _dot(p, h_ref[...]).astype(BF16)

        hg = hc_ref[pl.ds(r0, gs), :]
        a = _dot(hg, wg_ref[...])
        a = a * _sigmoid(a) * _dot(hg, wu_ref[...])
        y = _dot(a.astype(BF16), wd_ref[...])

        @pl.when(f == 0)
        def _():
            yc_ref[pl.ds(r0, gs), :] = y

        @pl.when(f != 0)
        def _():
            yc_ref[pl.ds(r0, gs), :] += y

        @pl.when(f == nf - 1)
        def _():
            gg = jnp.sum(jnp.where(hit, gt, 0.0), axis=1, keepdims=True)
            hi, lo = _split2(yc_ref[pl.ds(r0, gs), :] * gg)
            o_ref[...] += _dot_tn(jnp.concatenate([p, p], axis=0), jnp.concatenate([hi, lo], axis=0))

        return 0

    lax.fori_loop(0, n_groups, group, 0)

    @pl.when((e == pl.num_programs(2) - 1) & (f == nf - 1))
    def _():
        o_ref[...] = x_ref[...] + gate_ref[...] * o_ref[...]


def _moe(x, g, sc, sh, gate, router, wg, wu, wd, tf):
    B, L, D = x.shape
    E, _, F = wg.shape
    tm = _row_tile(L, 1024)
    assert tm % MOE_GROUP == 0
    return pl.pallas_call(
        functools.partial(_moe_kernel, tm=tm),
        grid=(B, L // tm, E, F // tf),
        in_specs=[pl.BlockSpec((None, tm, D), lambda b, i, e, f: (b, i, 0)),
                  pl.BlockSpec((1, D), lambda b, i, e, f: (0, 0)),
                  _mod_spec(sc, tm), _mod_spec(sh, tm), _mod_spec(gate, tm),
                  pl.BlockSpec(router.shape, lambda b, i, e, f: (0, 0)),
                  pl.BlockSpec((None, D, tf), lambda b, i, e, f: (e, 0, f)),
                  pl.BlockSpec((None, D, tf), lambda b, i, e, f: (e, 0, f)),
                  pl.BlockSpec((None, tf, D), lambda b, i, e, f: (e, f, 0))],
        out_specs=pl.BlockSpec((None, tm, D), lambda b, i, e, f: (b, i, 0)),
        out_shape=jax.ShapeDtypeStruct((B, L, D), F32),
        scratch_shapes=[pltpu.VMEM((tm, D), BF16), pltpu.VMEM((tm, LANES), F32),
                        pltpu.VMEM((LANES, tm), F32), pltpu.VMEM((LANES, tm), F32),
                        pltpu.VMEM((tm, D), BF16), pltpu.VMEM((tm, D), F32)],
        compiler_params=_cparams(("arbitrary",) * 4),
        name="expert_swiglu",
    )(x, g.reshape(1, D), sc, sh, gate, router, wg, wu, wd)


_P1_Q = 0
_P1_K = 1024
_P1_V = 1280
_P1_QI = 1536
_P1_KD = 2048
_P1_VD = 2560
_P1_KIWI = 3072
_P1_KID = 3200
_P1_N = 3328


def _rope(a, c, s1, s2):
    return a * c + pltpu.roll(a, LANES - ROPE_DIM // 2, 1) * s1 + pltpu.roll(a, ROPE_DIM // 2, 1) * s2


def _proj1_kernel(x_ref, g_ref, sc_ref, sh_ref, w_ref, wvt_ref, c_ref, s1_ref, s2_ref, qg_ref, kg_ref,
                  q_ref, k_ref, v_ref, qi_ref, kd_ref, vd_ref, kiwi_ref, kid_ref, *, vt):
    h = _modnorm(x_ref[...], g_ref[...], sc_ref[...], sh_ref[...]).astype(BF16)
    cos, s1, s2 = c_ref[...], s1_ref[...], s2_ref[...]
    rr = lax.broadcasted_iota(I32, (LANES, LANES), 0) // HEAD_DIM
    cc = lax.broadcasted_iota(I32, (LANES, LANES), 1) // HEAD_DIM
    bd = jnp.where(rr == cc, 1.0, 0.0).astype(BF16)
    lane = lax.broadcasted_iota(I32, (1, LANES), 1)

    def headnorm(a, gain):
        hi, lo = _split2(a * a)
        ss = _dot(hi, bd) + _dot(lo, bd)
        return a * lax.rsqrt(ss * (1.0 / HEAD_DIM) + EPS) * gain

    def group(c0, width):
        y = _dot(h, w_ref[:, c0:c0 + width])
        return [y[:, i * LANES:(i + 1) * LANES] for i in range(width // LANES)]

    qscale = HEAD_DIM ** -0.5
    for gidx in range(2):
        for i, a in enumerate(group(_P1_Q + gidx * 512, 512)):
            a = _rope(headnorm(a, qg_ref[...]), cos, s1, s2)
            cidx = gidx * 4 + i
            q_ref[:, cidx * LANES:(cidx + 1) * LANES] = (a * qscale).astype(BF16)
    kv = group(_P1_K, 512)
    for i in range(2):
        k_ref[:, i * LANES:(i + 1) * LANES] = _rope(headnorm(kv[i], kg_ref[...]), cos, s1, s2)
        v_ref[:, i * LANES:(i + 1) * LANES] = kv[2 + i]
    for i, a in enumerate(group(_P1_QI, 512)):
        qi_ref[:, i * LANES:(i + 1) * LANES] = (_rope(a, cos, s1, s2) * (HEAD_DIM ** -0.5)).astype(BF16)
    for i, a in enumerate(group(_P1_KD, 512)):
        kd_ref[:, i * LANES:(i + 1) * LANES] = _rope(headnorm(a, kg_ref[...]), cos, s1, s2).astype(BF16)
    if vt:
        vd_ref[...] = _dot_nt(wvt_ref[...], h).astype(BF16)
    else:
        for i, a in enumerate(group(_P1_VD, 512)):
            vd_ref[:, i * LANES:(i + 1) * LANES] = a.astype(BF16)
    kiwi, kid = group(_P1_KIWI, 256)
    first = lane < HEAD_DIM
    kiwi = _rope(kiwi, jnp.where(first, cos, 1.0), jnp.where(first, s1, 0.0), jnp.where(first, s2, 0.0))
    is_wi = (lane >= HEAD_DIM) & (lane < HEAD_DIM + IDX_HEADS)
    kiwi_ref[...] = kiwi * jnp.where(is_wi, IDX_HEADS ** -0.5, 1.0)
    kid_ref[...] = _rope(kid, cos, s1, s2).astype(BF16)


def _proj1(x, g, sc, sh, w, rope_tabs, qg, kg, vt):
    B, L, D = x.shape
    tm = _row_tile(L, 512)
    row = lambda d, dt: (pl.BlockSpec((None, tm, d), lambda b, i: (b, i, 0)), jax.ShapeDtypeStruct((B, L, d), dt))
    vd_out = row(512, BF16)
    if vt:
        vd_out = (pl.BlockSpec((None, 512, tm), lambda b, i: (b, 0, i)), jax.ShapeDtypeStruct((B, 512, L), BF16))
    outs = [row(1024, BF16), row(256, F32), row(256, F32), row(512, BF16), row(512, BF16), vd_out,
            row(LANES, F32), row(LANES, BF16)]
    wvt = w[:, _P1_VD:_P1_VD + 512].T
    tab = pl.BlockSpec((tm, LANES), lambda b, i: (i, 0))
    vec = pl.BlockSpec((1, LANES), lambda b, i: (0, 0))
    return pl.pallas_call(
        functools.partial(_proj1_kernel, vt=vt),
        grid=(B, L // tm),
        in_specs=[pl.BlockSpec((None, tm, D), lambda b, i: (b, i, 0)),
                  pl.BlockSpec((1, D), lambda b, i: (0, 0)),
                  _mod_spec(sc, tm), _mod_spec(sh, tm),
                  pl.BlockSpec(w.shape, lambda b, i: (0, 0)),
                  pl.BlockSpec(wvt.shape, lambda b, i: (0, 0)),
                  tab, tab, tab, vec, vec],
        out_specs=[o[0] for o in outs],
        out_shape=[o[1] for o in outs],
        compiler_params=_cparams(("arbitrary", "arbitrary")),
        name="l1_in_proj",
    )(x, g.reshape(1, D), sc, sh, w, wvt, *rope_tabs, qg, kg)


def _pad_l1_weight(w):
    D = w.shape[0]
    q, k, v, qi, ki, wi = jnp.split(w, [1024, 1280, 1536, 2048, 2112], axis=1)
    dup = lambda m, nh: jnp.repeat(m.reshape(D, nh, 1, HEAD_DIM), 2, axis=2).reshape(D, nh * 2 * HEAD_DIM)
    kiwi = jnp.concatenate([ki, wi, jnp.zeros((D, LANES - HEAD_DIM - IDX_HEADS), w.dtype)], axis=1)
    out = jnp.concatenate([q, k, v, qi, dup(k, DSA_KV_HEADS), dup(v, DSA_KV_HEADS), kiwi, dup(ki, 1)], axis=1)
    assert out.shape[1] == _P1_N
    return out.astype(BF16)


def _rope_tables(pos):
    half = ROPE_DIM // 2
    inv = 1.0 / (ROPE_THETA ** (jnp.arange(half, dtype=F32) / half))
    ang = pos.astype(F32)[:, None] * inv[None, :]
    cos, sin = jnp.cos(ang), jnp.sin(ang)
    n = pos.shape[0]
    one = jnp.ones((n, HEAD_DIM - ROPE_DIM), F32)
    zero = jnp.zeros((n, HEAD_DIM - ROPE_DIM), F32)
    z8 = jnp.zeros((n, half), F32)
    c = jnp.concatenate([cos, cos, one], axis=1)
    s1 = jnp.concatenate([-sin, z8, zero], axis=1)
    s2 = jnp.concatenate([z8, sin, zero], axis=1)
    return tuple(jnp.tile(t, (1, 2)) for t in (c, s1, s2))


def _sort_key(score):
    b = lax.bitcast_convert_type(score + 0.0, I32)
    return jnp.where(b < 0, b ^ 0x7FFFFFFF, b)


def _sort_key_const(value):
    b = int(np.float32(value).view(np.int32))
    return b ^ 0x7FFFFFFF if b < 0 else b


def _topk_kernel(qi_ref, kiwi_ref, kid_ref, mask_ref, key_ref, *, tq, tk, past, s_valid, topk, nkb_max, mask_t):
    i = pl.program_id(1)
    qbase = past + i * tq
    qrow = qbase + lax.broadcasted_iota(I32, (tq, 1), 0)
    adm_lim = jnp.minimum(((qrow >> 6) + 1) << 6, s_valid)
    lim = jnp.minimum((((qbase + tq - 1) >> 6) + 1) << 6, s_valid)
    nkb = jnp.minimum((lim + tk - 1) // tk, nkb_max)
    n_out = (jnp.zeros((tq, 1), I32) + (s_valid - jnp.minimum(nkb * tk, s_valid))).astype(F32)
    lane = lax.broadcasted_iota(I32, (1, LANES), 1)
    half = [lane < HEAD_DIM, lane >= HEAD_DIM]
    col = lax.broadcasted_iota(I32, (tq, tk), 1)
    negkey = _sort_key_const(NEG)
    kiwi = kiwi_ref[...]
    wi = [kiwi[:, HEAD_DIM + h:HEAD_DIM + h + 1] for h in range(IDX_HEADS)]
    qi = qi_ref[...]
    qim = []
    for h in range(IDX_HEADS):
        qc = qi[:, (h // 2) * LANES:(h // 2 + 1) * LANES]
        qim.append(jnp.where(half[h % 2], qc, jnp.zeros_like(qc)))

    def score_body(j, _):
        k0 = pl.multiple_of(j * tk, tk)
        kb = kid_ref[pl.ds(k0, tk), :]
        sc = jnp.zeros((tq, tk), F32)
        for h in range(IDX_HEADS):
            sc = sc + wi[h] * jnp.maximum(_dot_nt(qim[h], kb), 0.0)
        kpos = col + k0
        key = _sort_key(jnp.where(kpos < adm_lim, sc, NEG))
        key_ref[j] = jnp.where(kpos < s_valid, key, INT_MIN)
        return 0

    lax.fori_loop(0, nkb, score_body, 0)

    def count(thr, strict):
        def cbody(j, acc):
            kk = key_ref[j]
            for c in range(tk // LANES):
                kc = kk[:, c * LANES:(c + 1) * LANES]
                hit = (kc > thr) if strict else (kc >= thr)
                acc = acc + jnp.where(hit, 1.0, 0.0)
            return acc
        acc = lax.fori_loop(0, nkb, cbody, jnp.zeros((tq, LANES), F32))
        cnt = jnp.sum(acc, axis=-1, keepdims=True)
        out_hit = (negkey > thr) if strict else (negkey >= thr)
        return cnt + jnp.where(out_hit, n_out, 0.0)

    kf = float(topk)

    def bis_body(p, prefix):
        cand = prefix | jnp.left_shift(jnp.int32(1), 31 - p)
        cnt = count(cand ^ INT_MIN, False)
        return jnp.where(cnt >= kf, cand, prefix)

    prefix = lax.fori_loop(0, 32, bis_body, jnp.zeros((tq, 1), I32))
    tau = prefix ^ INT_MIN
    need = kf - count(tau, True)

    rr = lax.broadcasted_iota(I32, (tk, tk), 0)
    cc = lax.broadcasted_iota(I32, (tk, tk), 1)
    U = jnp.where(rr < cc, 1.0, 0.0).astype(BF16)

    def sel_body(j, carry):
        kk = key_ref[j]
        eq = jnp.where(kk == tau, 1.0, 0.0)
        rank = carry + _dot(eq.astype(BF16), U)
        take = jnp.where(kk > tau, 1.0, jnp.where(rank < need, eq, 0.0))
        kpos = col + j * tk
        mk = jnp.where(kpos < adm_lim, take, 0.0)
        mask_ref[j] = (mk.T if mask_t else mk).astype(mask_ref.dtype)
        return carry + jnp.sum(eq, axis=-1, keepdims=True)

    lax.fori_loop(0, nkb, sel_body, jnp.zeros((tq, 1), F32))

    def zero_body(j, _):
        mask_ref[j] = jnp.zeros(mask_ref.shape[1:], mask_ref.dtype)
        return 0

    lax.fori_loop(nkb, nkb_max, zero_body, 0)


def _topk_mask(qi, kiwi, kid, past, s_valid, tq, tk, mask_t):
    B, L, _ = qi.shape
    S = kid.shape[1]
    nb = S // tk
    topk = min(TOPK_MAX, s_valid // 4)
    kern = functools.partial(_topk_kernel, tq=tq, tk=tk, past=past, s_valid=s_valid, topk=topk, nkb_max=nb,
                             mask_t=mask_t)
    mshape = (tk, tq) if mask_t else (tq, tk)
    return pl.pallas_call(
        kern,
        grid=(B, L // tq),
        in_specs=[pl.BlockSpec((None, tq, qi.shape[-1]), lambda b, i: (b, i, 0)),
                  pl.BlockSpec((None, tq, LANES), lambda b, i: (b, i, 0)),
                  pl.BlockSpec((None, S, LANES), lambda b, i: (b, 0, 0))],
        out_specs=pl.BlockSpec((None, None, nb) + mshape, lambda b, i: (b, i, 0, 0, 0)),
        out_shape=jax.ShapeDtypeStruct((B, L // tq, nb) + mshape, BF16),
        scratch_shapes=[pltpu.VMEM((nb, tq, tk), I32)],
        compiler_params=_cparams(("arbitrary", "arbitrary")),
        name="indexer_topk",
    )(qi, kiwi, kid)


DSA_CHAINS = 2


def _dsa_kernel(q_ref, kd_ref, vd_ref, mask_ref, o_ref, m_ref, l_ref, acc_ref, *, tq, tk, past, s_valid, nkb_max):
    i = pl.program_id(1)
    qbase = past + i * tq
    lim = jnp.minimum((((qbase + tq - 1) >> 6) + 1) << 6, s_valid)
    nkb = jnp.minimum((lim + tk - 1) // tk, nkb_max)
    lane = lax.broadcasted_iota(I32, (1, LANES), 1)
    first = lane < HEAD_DIM
    hpp = m_ref.shape[0]
    for j0 in range(0, DSA_KV_HEADS, hpp):
        q4s = []
        for j in range(j0, j0 + hpp):
            qs = []
            for c in (2 * j, 2 * j + 1):
                qc = q_ref[:, c * LANES:(c + 1) * LANES]
                qs += [jnp.where(first, qc, jnp.zeros_like(qc)), jnp.where(first, jnp.zeros_like(qc), qc)]
            q4s.append(jnp.concatenate(qs, axis=0))
        m_ref[...] = jnp.full(m_ref.shape, NEG, F32)
        l_ref[...] = jnp.zeros_like(l_ref)
        acc_ref[...] = jnp.zeros_like(acc_ref)

        def body(jb, _):
            k0 = pl.multiple_of(jb * tk, tk)
            mk = mask_ref[jb].astype(F32)
            sel4 = jnp.concatenate([mk] * 4, axis=0) > 0.5
            for jj in range(hpp):
                j = j0 + jj
                kb = kd_ref[pl.ds(k0, tk), j * LANES:(j + 1) * LANES]
                vb = vd_ref[pl.ds(k0, tk), j * LANES:(j + 1) * LANES]
                s = jnp.where(sel4, _dot_nt(q4s[jj], kb), -jnp.inf)
                m_old = m_ref[jj]
                m_new = jnp.maximum(m_old, jnp.max(s, axis=-1, keepdims=True))
                alpha = jnp.exp(m_old - m_new)
                p = jnp.exp(s - m_new)
                l_ref[jj] = alpha * l_ref[jj] + jnp.sum(p, axis=-1, keepdims=True)
                acc_ref[jj] = alpha * acc_ref[jj] + _dot(p.astype(BF16), vb)
                m_ref[jj] = m_new
            return 0

        lax.fori_loop(0, nkb, body, 0)
        for jj in range(hpp):
            o4 = acc_ref[jj] / l_ref[jj]
            for cc in range(2):
                oc = jnp.where(first, o4[(2 * cc) * tq:(2 * cc + 1) * tq], o4[(2 * cc + 1) * tq:(2 * cc + 2) * tq])
                c = 2 * (j0 + jj) + cc
                o_ref[:, c * LANES:(c + 1) * LANES] = oc.astype(o_ref.dtype)


def _dsa_t_kernel(q_ref, kd_ref, vt_ref, mask_ref, o_ref, acc_ref, *, tq, tk, past, s_valid, nkb_max):
    i = pl.program_id(1)
    qbase = past + i * tq
    lim = jnp.minimum((((qbase + tq - 1) >> 6) + 1) << 6, s_valid)
    nkb = jnp.minimum((lim + tk - 1) // tk, nkb_max)
    lane = lax.broadcasted_iota(I32, (1, LANES), 1)
    first = lane < HEAD_DIM
    nch = acc_ref.shape[0]
    for j0 in range(0, DSA_KV_HEADS, nch):
        q4s = []
        for j in range(j0, j0 + nch):
            qs = []
            for c in (2 * j, 2 * j + 1):
                qc = q_ref[:, c * LANES:(c + 1) * LANES]
                qs += [jnp.where(first, qc, jnp.zeros_like(qc)), jnp.where(first, jnp.zeros_like(qc), qc)]
            q4s.append(jnp.concatenate(qs, axis=0))
        acc_ref[...] = jnp.zeros_like(acc_ref)

        def body(jb, carry):
            k0 = pl.multiple_of(jb * tk, tk)
            nsub = tk // mask_ref.shape[1]
            mk = jnp.concatenate([mask_ref[jb * nsub + t] for t in range(nsub)], axis=0).astype(F32)
            sel4 = jnp.concatenate([mk] * 4, axis=1) > 0.5
            out = []
            for jj in range(nch):
                j = j0 + jj
                m_old, l_old = carry[2 * jj], carry[2 * jj + 1]
                kb = kd_ref[pl.ds(k0, tk), j * LANES:(j + 1) * LANES]
                vb = vt_ref[j * LANES:(j + 1) * LANES, pl.ds(k0, tk)]
                s = jnp.where(sel4, _dot_nt(kb, q4s[jj]), -jnp.inf)
                m_new = jnp.maximum(m_old, jnp.max(s, axis=0, keepdims=True))
                alpha = jnp.exp(m_old - m_new)
                p = jnp.exp(s - m_new)
                l_new = alpha * l_old + jnp.sum(p, axis=0, keepdims=True)
                acc_ref[jj] = alpha * acc_ref[jj] + _dot(vb, p.astype(BF16))
                out += [m_new, l_new]
            return tuple(out)

        init = (jnp.full((1, 4 * tq), NEG, F32), jnp.zeros((1, 4 * tq), F32)) * nch
        stats = lax.fori_loop(0, nkb, body, init)
        for jj in range(nch):
            o4 = acc_ref[jj] / stats[2 * jj + 1]
            heads = [o4[:, g * tq:(g + 1) * tq].T for g in range(4)]
            for cc in range(2):
                c = 2 * (j0 + jj) + cc
                o_ref[:, c * LANES:(c + 1) * LANES] = jnp.where(first, heads[2 * cc], heads[2 * cc + 1]).astype(o_ref.dtype)


def _dsa_attention_t(q, kd, vt, mask_t, past, s_valid, tq, tk):
    B, L, W = q.shape
    S = kd.shape[1]
    nb = S // tk
    assert tq == LANES and tk % mask_t.shape[3] == 0 and S % tk == 0
    kern = functools.partial(_dsa_t_kernel, tq=tq, tk=tk, past=past, s_valid=s_valid, nkb_max=nb)
    return pl.pallas_call(
        kern,
        grid=(B, L // tq),
        in_specs=[pl.BlockSpec((None, tq, W), lambda b, i: (b, i, 0)),
                  pl.BlockSpec((None, S, kd.shape[-1]), lambda b, i: (b, 0, 0)),
                  pl.BlockSpec((None, vt.shape[1], S), lambda b, i: (b, 0, 0)),
                  pl.BlockSpec((None, None) + mask_t.shape[2:], lambda b, i: (b, i, 0, 0, 0))],
        out_specs=pl.BlockSpec((None, tq, W), lambda b, i: (b, i, 0)),
        out_shape=jax.ShapeDtypeStruct((B, L, W), BF16),
        scratch_shapes=[pltpu.VMEM((DSA_CHAINS, LANES, 4 * tq), F32)],
        compiler_params=_cparams(("arbitrary", "arbitrary")),
        name="sparse_attn_t",
    )(q, kd, vt, mask_t)


def _dsa_attention(q, kd, vd, mask, past, s_valid, tq, tk):
    B, L, W = q.shape
    S = kd.shape[1]
    nb = S // tk
    kern = functools.partial(_dsa_kernel, tq=tq, tk=tk, past=past, s_valid=s_valid, nkb_max=nb)
    return pl.pallas_call(
        kern,
        grid=(B, L // tq),
        in_specs=[pl.BlockSpec((None, tq, W), lambda b, i: (b, i, 0)),
                  pl.BlockSpec((None, S, kd.shape[-1]), lambda b, i: (b, 0, 0)),
                  pl.BlockSpec((None, S, vd.shape[-1]), lambda b, i: (b, 0, 0)),
                  pl.BlockSpec((None, None, nb, tq, tk), lambda b, i: (b, i, 0, 0, 0))],
        out_specs=pl.BlockSpec((None, tq, W), lambda b, i: (b, i, 0)),
        out_shape=jax.ShapeDtypeStruct((B, L, W), BF16),
        scratch_shapes=[pltpu.VMEM((DSA_CHAINS, 4 * tq, 1), F32), pltpu.VMEM((DSA_CHAINS, 4 * tq, 1), F32),
                        pltpu.VMEM((DSA_CHAINS, 4 * tq, LANES), F32)],
        compiler_params=_cparams(("arbitrary", "arbitrary")),
        name="sparse_attn",
    )(q, kd, vd, mask)


def _pad_rows(a, mult):
    s = a.shape[1]
    sp = -(-s // mult) * mult
    if sp == s:
        return a
    return jnp.concatenate([a, jnp.zeros((a.shape[0], sp - s) + a.shape[2:], a.dtype)], axis=1)


def _dup_heads(a):
    B, S, H, d = a.shape
    return jnp.repeat(a[:, :, :, None, :], 2, axis=3).reshape(B, S, H * 2 * d)


SB_TQ, SB_TK = 1024, 256
SB_TK_DECODE = 512
IDX_TK = 512
DSA_TK = 1024


def _trunk(x, mods0, mods1, past, W, flat):
    B, L, D = x.shape
    past_len = 0 if past is None else past[0].shape[1]

    def tok(a):
        return a.reshape(1, B * L, a.shape[-1]) if flat else a

    def untok(a):
        return a.reshape(B, L, a.shape[-1]) if flat else a

    def mod(m):
        if flat:
            return jnp.repeat(m, L, axis=0).reshape(1, B * L, D)
        return m[:, None, :]

    sh_m, sc_m, g_m, sh_f, sc_f, g_f = [mod(m) for m in mods0]
    xt = tok(x)

    q, k, v, kb, vb, u = _proj0(xt, W['l0_norm_mix'], sc_m, sh_m, W['l0_w_in'])
    q, k, v, kb, vb, u = [untok(a) for a in (q, k, v, kb, vb, u)]
    if past is None:
        tq, tk = _row_tile(L, SB_TQ), min(L, SB_TK)
        k_all, v_all = kb, vb
        prev = jnp.zeros((B, 32, u.shape[-1]), F32)
    else:
        tq, tk = L, SB_TK_DECODE
        k_all = _pad_rows(jnp.concatenate([past[0].reshape(B, past_len, -1).astype(BF16), kb], axis=1), tk)
        v_all = _pad_rows(jnp.concatenate([past[1].reshape(B, past_len, -1).astype(BF16), vb], axis=1), tk)
        prev = jnp.concatenate([jnp.zeros((B, 2, u.shape[-1]), F32), past[2]], axis=1)
    o_a = _sb_attention(q, k_all, v_all, past_len, tq, tk)
    o_b = _conv_module(u, prev, W['l0_conv_w'], W['l0_conv_b'], W['l0_conv_ln_g'], W['l0_conv_ln_b'])
    conv_state = jnp.concatenate([prev[:, 2:], u], axis=1)[:, -(CONV_WIDTH - 1):]
    xt = _outproj(xt, g_m, [tok(o_a), tok(o_b)], [W['l0_w_out'][:512], W['l0_w_out'][512:]])
    xt = _ffn(xt, W['l0_norm_ff'], sc_f, sh_f, g_f, W['l0_ff_wg'], W['l0_ff_wu'], W['l0_ff_wd'], 1408)

    sh_m, sc_m, g_m, sh_f, sc_f, g_f = [mod(m) for m in mods1]
    pos = past_len + jnp.arange(L, dtype=I32)
    tabs = _rope_tables(pos)
    if flat:
        tabs = tuple(jnp.tile(t, (B, 1)) for t in tabs)
    qg = jnp.tile(W['l1_q_norm'].reshape(1, HEAD_DIM), (1, 2))
    kg = jnp.tile(W['l1_k_norm'].reshape(1, HEAD_DIM), (1, 2))
    keys_on_sublanes = past is None and L % LANES == 0
    q1, k1, v1, qi, kd, vd, kiwi, kid = _proj1(xt, W['l1_norm_mix'], sc_m, sh_m, W['l1_w_in'], tabs, qg, kg,
                                               vt=keys_on_sublanes)
    q1, k1, v1, qi, kd, kiwi, kid = [untok(a) for a in (q1, k1, v1, qi, kd, kiwi, kid)]
    if not keys_on_sublanes:
        vd = untok(vd)
    s_valid = past_len + L
    tq1 = min(L, LANES)
    tk1 = min(L, IDX_TK) if past is None else IDX_TK
    if past is not None:
        kd = _pad_rows(jnp.concatenate([_dup_heads(past[3]).astype(BF16), kd], axis=1), tk1)
        vd = _pad_rows(jnp.concatenate([_dup_heads(past[4]).astype(BF16), vd], axis=1), tk1)
        kid = _pad_rows(jnp.concatenate([_dup_heads(past[5][:, :, None, :]).astype(BF16), kid], axis=1), tk1)
    mask = _topk_mask(qi, kiwi, kid, past_len, s_valid, tq1, tk1, mask_t=keys_on_sublanes)
    if keys_on_sublanes:
        tk_attn = DSA_TK if kd.shape[1] % DSA_TK == 0 else tk1
        o1 = _dsa_attention_t(q1, kd, vd, mask, past_len, s_valid, tq1, tk_attn)
    else:
        o1 = _dsa_attention(q1, kd, vd, mask, past_len, s_valid, tq1, tk1)
    xt = _outproj(xt, g_m, [tok(o1)], [W['l1_w_out']])
    xt = _moe(xt, W['l1_norm_ff'], sc_f, sh_f, g_f, W['l1_router'], W['l1_exp_wg'], W['l1_exp_wu'],
              W['l1_exp_wd'], 1792)

    hd = HEAD_DIM
    states = (k.reshape(B, L, SB_HEADS, hd), v.reshape(B, L, SB_HEADS, hd), conv_state,
              k1.reshape(B, L, DSA_KV_HEADS, hd), v1.reshape(B, L, DSA_KV_HEADS, hd), kiwi[..., :hd])
    return untok(xt), states


def kernel(x_prompt, x_sample, c_prompt, c_sample, cache_sb_k, cache_sb_v, cache_conv, cache_dsa_k, cache_dsa_v, cache_dsa_kidx, l0_ada_w, l0_ada_b, l0_norm_mix, l0_w_in, l0_conv_w, l0_conv_b, l0_conv_ln_g, l0_conv_ln_b, l0_w_out, l0_norm_ff, l0_ff_wg, l0_ff_wu, l0_ff_wd, l1_ada_w, l1_ada_b, l1_norm_mix, l1_w_in, l1_q_norm, l1_k_norm, l1_w_out, l1_norm_ff, l1_router, l1_exp_wg, l1_exp_wu, l1_exp_wd):
    D = x_prompt.shape[-1]
    router = jnp.concatenate([l1_router, jnp.zeros((D, LANES - N_EXPERTS), F32)], axis=1).astype(BF16)
    W = dict(l0_norm_mix=l0_norm_mix, l0_w_in=l0_w_in.astype(BF16), l0_conv_w=l0_conv_w, l0_conv_b=l0_conv_b,
             l0_conv_ln_g=l0_conv_ln_g, l0_conv_ln_b=l0_conv_ln_b, l0_w_out=l0_w_out.astype(BF16),
             l0_norm_ff=l0_norm_ff, l0_ff_wg=l0_ff_wg.astype(BF16), l0_ff_wu=l0_ff_wu.astype(BF16),
             l0_ff_wd=l0_ff_wd.astype(BF16), l1_norm_mix=l1_norm_mix, l1_w_in=_pad_l1_weight(l1_w_in),
             l1_q_norm=l1_q_norm, l1_k_norm=l1_k_norm, l1_w_out=l1_w_out.astype(BF16), l1_norm_ff=l1_norm_ff,
             l1_router=router, l1_exp_wg=l1_exp_wg.astype(BF16), l1_exp_wu=l1_exp_wu.astype(BF16),
             l1_exp_wd=l1_exp_wd.astype(BF16))
    bp = c_prompt.shape[0]
    c_all = jnp.concatenate([c_prompt, c_sample], axis=0)
    m0 = _ada(c_all, l0_ada_w, l0_ada_b)
    m1 = _ada(c_all, l1_ada_w, l1_ada_b)
    mods = lambda m, sl: [t[sl] for t in jnp.split(m, 6, axis=-1)]
    y_p, st_p = _trunk(x_prompt, mods(m0, slice(0, bp)), mods(m1, slice(0, bp)), None, W, flat=False)
    past = (cache_sb_k, cache_sb_v, cache_conv, cache_dsa_k, cache_dsa_v, cache_dsa_kidx)
    y_s, st_s = _trunk(x_sample, mods(m0, slice(bp, None)), mods(m1, slice(bp, None)), past, W, flat=True)
    return (y_p, y_s) + tuple(st_p) + tuple(st_s)
```

```python
import functools
import math

import numpy as np
import jax
import jax.numpy as jnp
from jax import lax
from jax.experimental import pallas as pl
from jax.experimental.pallas import tpu as pltpu

F32 = jnp.float32
BF16 = jnp.bfloat16
I32 = jnp.int32

EPS = 1e-6
NEG = -1e30
HEAD_DIM = 64
CHUNK = 64
TOPK_MAX = 256
ROPE_DIM = 16
ROPE_THETA = 500000.0
CONV_WIDTH = 31
SB_HEADS = 8
DSA_HEADS = 16
DSA_KV_HEADS = 4
IDX_HEADS = 8
N_EXPERTS = 8
LANES = 128
INT_MIN = -2 ** 31
LOG2E = math.log2(math.e)
VMEM_LIMIT = 56 * 2 ** 20


def _cparams(sem):
    return pltpu.CompilerParams(dimension_semantics=sem, vmem_limit_bytes=VMEM_LIMIT)


def _dot(a, b):
    return jnp.dot(a, b, preferred_element_type=F32)


def _dot_nt(a, b):
    return lax.dot_general(a, b, (((1,), (1,)), ((), ())), preferred_element_type=F32)


def _split2(x):
    hi = x.astype(BF16)
    lo = (x - hi.astype(F32)).astype(BF16)
    return hi, lo


def _sigmoid(x):
    return 1.0 / (1.0 + jnp.exp(-x))


def _modnorm(x, g, sc, sh):
    ms = jnp.mean(x * x, axis=-1, keepdims=True)
    y = x * lax.rsqrt(ms + EPS) * g
    return y * (1.0 + sc) + sh


def _mod_spec(mod, tm):
    if mod.shape[1] == 1:
        return pl.BlockSpec((None, 1, mod.shape[2]), lambda b, i, *_: (b, 0, 0))
    return pl.BlockSpec((None, tm, mod.shape[2]), lambda b, i, *_: (b, i, 0))


def _row_tile(L, pref):
    return pref if L % pref == 0 else L


def _ada_kernel(c_ref, w_ref, b_ref, o_ref):
    c = c_ref[...]
    s = c * _sigmoid(c)
    s_hi, s_lo = _split2(s)
    w_hi, w_lo = _split2(w_ref[...])
    o_ref[...] = _dot(s_hi, w_hi) + _dot(s_lo, w_hi) + _dot(s_hi, w_lo) + b_ref[...]


def _ada(c, w, b):
    bc, d = c.shape
    n = w.shape[1]
    tn = 512
    return pl.pallas_call(
        _ada_kernel,
        grid=(n // tn,),
        in_specs=[pl.BlockSpec((bc, d), lambda j: (0, 0)),
                  pl.BlockSpec((d, tn), lambda j: (0, j)),
                  pl.BlockSpec((1, tn), lambda j: (0, j))],
        out_specs=pl.BlockSpec((bc, tn), lambda j: (0, j)),
        out_shape=jax.ShapeDtypeStruct((bc, n), F32),
        compiler_params=_cparams(("arbitrary",)),
        name="ada_mod",
    )(c, w, b.reshape(1, n))


def _proj0_kernel(x_ref, g_ref, sc_ref, sh_ref, w_ref, q_ref, k_ref, v_ref, kb_ref, vb_ref, u_ref):
    h = _modnorm(x_ref[...], g_ref[...], sc_ref[...], sh_ref[...]).astype(BF16)
    wd = q_ref.shape[-1]

    def mm(c):
        return _dot(h, w_ref[:, c * wd:(c + 1) * wd])

    q_ref[...] = (mm(0) * (HEAD_DIM ** -0.5)).astype(BF16)
    k = mm(1)
    k_ref[...] = k
    kb_ref[...] = k.astype(BF16)
    v = mm(2)
    v_ref[...] = v
    vb_ref[...] = v.astype(BF16)
    u_ref[...] = mm(3) * _sigmoid(mm(4))


def _proj0(x, g, sc, sh, w):
    B, L, D = x.shape
    wd = SB_HEADS * HEAD_DIM
    tm = _row_tile(L, 512)
    row = lambda d, dt: (pl.BlockSpec((None, tm, d), lambda b, i: (b, i, 0)), jax.ShapeDtypeStruct((B, L, d), dt))
    outs = [row(wd, BF16), row(wd, F32), row(wd, F32), row(wd, BF16), row(wd, BF16), row(wd, F32)]
    return pl.pallas_call(
        _proj0_kernel,
        grid=(B, L // tm),
        in_specs=[pl.BlockSpec((None, tm, D), lambda b, i: (b, i, 0)),
                  pl.BlockSpec((1, D), lambda b, i: (0, 0)),
                  _mod_spec(sc, tm), _mod_spec(sh, tm),
                  pl.BlockSpec(w.shape, lambda b, i: (0, 0))],
        out_specs=[o[0] for o in outs],
        out_shape=[o[1] for o in outs],
        compiler_params=_cparams(("arbitrary", "arbitrary")),
        name="l0_in_proj",
    )(x, g.reshape(1, D), sc, sh, w)


SB_EXP_UNDERFLOW = -120.0


def _sb_kernel(q_ref, k_ref, v_ref, o_ref, *, tq, tk, past, nkb_max):
    i = pl.program_id(2)
    qbase = past + i * tq
    nkb = jnp.minimum((qbase + tq - 2 + tk) // tk, nkb_max)
    lane = lax.broadcasted_iota(I32, (1, LANES), 1)
    half = [lane < HEAD_DIM, lane >= HEAD_DIM]
    q = q_ref[...]
    qm = [jnp.where(half[hh], q, jnp.zeros_like(q)) for hh in range(2)]
    rr = lax.broadcasted_iota(I32, (tk, tk), 0)
    cc = lax.broadcasted_iota(I32, (tk, tk), 1)
    U = jnp.where(rr > cc, 1.0, 0.0).astype(BF16)
    dcol = lax.broadcasted_iota(I32, (tq, tk), 1) - lax.broadcasted_iota(I32, (tq, tk), 0)

    def make_body(masked):
        def body(jj, carry):
            r0, r1, acc = carry
            j = nkb - 1 - jj
            k0 = pl.multiple_of(j * tk, tk)
            kb = k_ref[pl.ds(k0, tk), :]
            vb = v_ref[pl.ds(k0, tk), :]
            mask = dcol < (qbase - k0)
            rs = [r0, r1]
            for hh in range(2):
                z = _dot_nt(qm[hh], kb)
                sp = jnp.log(1.0 + jnp.exp(-jnp.abs(z)))
                lb = jnp.minimum(z, 0.0) - sp
                l1m = -jnp.maximum(z, 0.0) - sp
                if masked:
                    l1m = jnp.where(mask, l1m, 0.0)
                hi, lo = _split2(l1m)
                cs = _dot(hi, U) + _dot(lo, U)
                w = jnp.exp(lb + cs + rs[hh])
                if masked:
                    w = jnp.where(mask, w, 0.0)
                vm = jnp.where(half[hh], vb, jnp.zeros_like(vb))
                acc = acc + _dot(w.astype(BF16), vm)
                rs[hh] = rs[hh] + cs[:, 0:1] + l1m[:, 0:1]
            return rs[0], rs[1], acc
        return body

    n_full = jnp.minimum(qbase // tk, nkb)
    z1 = jnp.zeros((tq, 1), F32)
    carry = lax.fori_loop(0, nkb - n_full, make_body(True), (z1, z1, jnp.zeros((tq, LANES), F32)))
    unmasked = make_body(False)

    def live(r0, r1):
        return jnp.max(jnp.maximum(r0, r1)) > SB_EXP_UNDERFLOW

    def w_cond(c):
        return (c[0] < nkb) & c[1]

    def w_body(c):
        r0, r1, acc = unmasked(c[0], c[2:])
        return c[0] + 1, live(r0, r1), r0, r1, acc

    out = lax.while_loop(w_cond, w_body, (nkb - n_full, live(carry[0], carry[1])) + tuple(carry))
    o_ref[...] = out[4].astype(o_ref.dtype)


def _sb_attention(q, k, v, past, tq, tk):
    B, L, W = q.shape
    S = k.shape[1]
    hp = W // LANES
    kern = functools.partial(_sb_kernel, tq=tq, tk=tk, past=past, nkb_max=S // tk)
    return pl.pallas_call(
        kern,
        grid=(B, hp, L // tq),
        in_specs=[pl.BlockSpec((None, tq, LANES), lambda b, h, i: (b, i, h)),
                  pl.BlockSpec((None, S, LANES), lambda b, h, i: (b, 0, h)),
                  pl.BlockSpec((None, S, LANES), lambda b, h, i: (b, 0, h))],
        out_specs=pl.BlockSpec((None, tq, LANES), lambda b, h, i: (b, i, h)),
        out_shape=jax.ShapeDtypeStruct((B, L, W), BF16),
        compiler_params=_cparams(("arbitrary", "arbitrary", "arbitrary")),
        name="stickbreak_attn",
    )(q, k, v)


def _conv_kernel(*refs, tm, has_halo):
    if has_halo:
        u_ref, halo_ref, prev_ref, w_ref, b_ref, g_ref, be_ref, o_ref, buf_ref, cv_ref = refs
    else:
        u_ref, prev_ref, w_ref, b_ref, g_ref, be_ref, o_ref, buf_ref, cv_ref = refs
    i = pl.program_id(1)
    pad = 32
    if has_halo:
        buf_ref[0:pad, :] = jnp.where(i == 0, prev_ref[...], halo_ref[...])
    else:
        buf_ref[0:pad, :] = prev_ref[...]
    buf_ref[pad:pad + tm, :] = u_ref[...]
    C = u_ref.shape[-1]
    rt = min(tm, 64)
    off = pad - (CONV_WIDTH - 1)
    for c in range(C // LANES):
        cs = slice(c * LANES, (c + 1) * LANES)
        for r in range(tm // rt):
            acc = jnp.zeros((rt, LANES), F32)
            for j in range(CONV_WIDTH):
                acc = acc + buf_ref[r * rt + off + j: r * rt + off + j + rt, cs] * w_ref[j:j + 1, cs]
            cv_ref[r * rt:(r + 1) * rt, cs] = acc
    conv = cv_ref[...] + b_ref[...]
    mu = jnp.mean(conv, axis=-1, keepdims=True)
    xc = conv - mu
    var = jnp.mean(xc * xc, axis=-1, keepdims=True)
    y = xc * lax.rsqrt(var + EPS) * g_ref[...] + be_ref[...]
    o_ref[...] = (y * _sigmoid(y)).astype(o_ref.dtype)


def _conv_module(u, prev32, w, b, g, be):
    B, L, C = u.shape
    tm = _row_tile(L, 256)
    has_halo = L > tm
    kern = functools.partial(_conv_kernel, tm=tm, has_halo=has_halo)
    in_specs = [pl.BlockSpec((None, tm, C), lambda b_, i: (b_, i, 0))]
    args = [u]
    if has_halo:
        r = tm // 32
        in_specs.append(pl.BlockSpec((None, 32, C), lambda b_, i: (b_, jnp.maximum(i * r - 1, 0), 0)))
        args.append(u)
    vec = pl.BlockSpec((1, C), lambda b_, i: (0, 0))
    in_specs += [pl.BlockSpec((None, 32, C), lambda b_, i: (b_, 0, 0)),
                 pl.BlockSpec((32, C), lambda b_, i: (0, 0)), vec, vec, vec]
    wpad = jnp.concatenate([w, jnp.zeros((32 - CONV_WIDTH, C), F32)], axis=0)
    args += [prev32, wpad, b.reshape(1, C), g.reshape(1, C), be.reshape(1, C)]
    return pl.pallas_call(
        kern,
        grid=(B, L // tm),
        in_specs=in_specs,
        out_specs=pl.BlockSpec((None, tm, C), lambda b_, i: (b_, i, 0)),
        out_shape=jax.ShapeDtypeStruct((B, L, C), BF16),
        scratch_shapes=[pltpu.VMEM((tm + 32, C), F32), pltpu.VMEM((tm, C), F32)],
        compiler_params=_cparams(("arbitrary", "arbitrary")),
        name="conv_module",
    )(*args)


def _outproj_kernel(*refs, n_in):
    x_ref, g_ref = refs[0], refs[1]
    a_refs = refs[2:2 + n_in]
    w_refs = refs[2 + n_in:2 + 2 * n_in]
    o_ref = refs[2 + 2 * n_in]
    acc = _dot(a_refs[0][...], w_refs[0][...])
    for a_ref, w_ref in zip(a_refs[1:], w_refs[1:]):
        acc = acc + _dot(a_ref[...], w_ref[...])
    o_ref[...] = x_ref[...] + g_ref[...] * acc


def _outproj(x, gate, acts, ws):
    B, L, D = x.shape
    tm = _row_tile(L, 512)
    n_in = len(acts)
    in_specs = [pl.BlockSpec((None, tm, D), lambda b, i: (b, i, 0)), _mod_spec(gate, tm)]
    in_specs += [pl.BlockSpec((None, tm, a.shape[-1]), lambda b, i: (b, i, 0)) for a in acts]
    in_specs += [pl.BlockSpec(w.shape, lambda b, i: (0, 0)) for w in ws]
    return pl.pallas_call(
        functools.partial(_outproj_kernel, n_in=n_in),
        grid=(B, L // tm),
        in_specs=in_specs,
        out_specs=pl.BlockSpec((None, tm, D), lambda b, i: (b, i, 0)),
        out_shape=jax.ShapeDtypeStruct((B, L, D), F32),
        compiler_params=_cparams(("arbitrary", "arbitrary")),
        name="out_proj",
    )(x, gate, *acts, *ws)


def _ffn_kernel(x_ref, g_ref, sc_ref, sh_ref, gate_ref, wg_ref, wu_ref, wd_ref, o_ref, h_ref, acc_ref):
    f = pl.program_id(2)

    @pl.when(f == 0)
    def _():
        h_ref[...] = _modnorm(x_ref[...], g_ref[...], sc_ref[...], sh_ref[...]).astype(BF16)
        acc_ref[...] = jnp.zeros_like(acc_ref)

    h = h_ref[...]
    a = _dot(h, wg_ref[...])
    a = a * _sigmoid(a) * _dot(h, wu_ref[...])
    acc_ref[...] += _dot(a.astype(BF16), wd_ref[...])

    @pl.when(f == pl.num_programs(2) - 1)
    def _():
        o_ref[...] = x_ref[...] + gate_ref[...] * acc_ref[...]


def _ffn(x, g, sc, sh, gate, wg, wu, wd, tf):
    B, L, D = x.shape
    F = wg.shape[1]
    tm = _row_tile(L, 512)
    return pl.pallas_call(
        _ffn_kernel,
        grid=(B, L // tm, F // tf),
        in_specs=[pl.BlockSpec((None, tm, D), lambda b, i, f: (b, i, 0)),
                  pl.BlockSpec((1, D), lambda b, i, f: (0, 0)),
                  _mod_spec(sc, tm), _mod_spec(sh, tm), _mod_spec(gate, tm),
                  pl.BlockSpec((D, tf), lambda b, i, f: (0, f)),
                  pl.BlockSpec((D, tf), lambda b, i, f: (0, f)),
                  pl.BlockSpec((tf, D), lambda b, i, f: (f, 0))],
        out_specs=pl.BlockSpec((None, tm, D), lambda b, i, f: (b, i, 0)),
        out_shape=jax.ShapeDtypeStruct((B, L, D), F32),
        scratch_shapes=[pltpu.VMEM((tm, D), BF16), pltpu.VMEM((tm, D), F32)],
        compiler_params=_cparams(("arbitrary", "arbitrary", "arbitrary")),
        name="dense_swiglu",
    )(x, g.reshape(1, D), sc, sh, gate, wg, wu, wd)


def _dot_tn(a, b):
    return lax.dot_general(a, b, (((0,), (0,)), ((), ())), preferred_element_type=F32)


MOE_GROUP = 128
RANK_BLOCK = 256


def _moe_kernel(x_ref, g_ref, sc_ref, sh_ref, gate_ref, r_ref, wg_ref, wu_ref, wd_ref, o_ref,
                h_ref, rk_ref, rankT_ref, dgT_ref, hc_ref, yc_ref, *, tm):
    e = pl.program_id(2)
    f = pl.program_id(3)
    nf = pl.num_programs(3)
    gs = MOE_GROUP
    lane = lax.broadcasted_iota(I32, (1, LANES), 1)

    @pl.when((e == 0) & (f == 0))
    def _():
        h = _modnorm(x_ref[...], g_ref[...], sc_ref[...], sh_ref[...]).astype(BF16)
        h_ref[...] = h
        o_ref[...] = jnp.zeros_like(o_ref)
        logits = jnp.where(lane < N_EXPERTS, _dot(h, r_ref[...]), -jnp.inf)
        m1 = jnp.max(logits, axis=-1, keepdims=True)
        i1 = jnp.min(jnp.where(logits == m1, lane, LANES), axis=-1, keepdims=True)
        rest = jnp.where(lane == i1, -jnp.inf, logits)
        m2 = jnp.max(rest, axis=-1, keepdims=True)
        i2 = jnp.min(jnp.where(rest == m2, lane, LANES), axis=-1, keepdims=True)
        e2 = jnp.exp(m2 - m1)
        den = 1.0 + e2
        dg = jnp.where(lane == i1, 1.0 / den, 0.0) + jnp.where(lane == i2, e2 / den, 0.0)
        sel = jnp.where(lane == i1, 1.0, 0.0) + jnp.where(lane == i2, 1.0, 0.0)
        rb = min(RANK_BLOCK, tm)
        rr = lax.broadcasted_iota(I32, (rb, rb), 0)
        cc = lax.broadcasted_iota(I32, (rb, rb), 1)
        lower = jnp.where(cc < rr, 1.0, 0.0).astype(BF16)
        carry = jnp.zeros((1, LANES), F32)
        for b in range(tm // rb):
            sb = sel[b * rb:(b + 1) * rb]
            rk_ref[b * rb:(b + 1) * rb, :] = carry + _dot(lower, sb.astype(BF16))
            carry = carry + jnp.sum(sb, axis=0, keepdims=True)
        rank = jnp.where(sel > 0.5, rk_ref[...], -1.0)
        rankT_ref[...] = rank.T
        dgT_ref[...] = dg.T

    rk = rankT_ref[pl.ds(e, 1), :]
    gt = dgT_ref[pl.ds(e, 1), :]
    cnt = jnp.sum(jnp.where(rk >= 0.0, 1.0, 0.0), axis=1, keepdims=True).astype(I32)[0, 0]
    n_groups = (cnt + gs - 1) // gs
    row = lax.broadcasted_iota(I32, (gs, 1), 0)

    def group(gi, _):
        r0 = pl.multiple_of(gi * gs, gs)
        hit = rk == (row + r0).astype(F32)
        p = jnp.where(hit, 1.0, 0.0).astype(BF16)

        @pl.when(f == 0)
        def _():
            hc_ref[pl.ds(r0, gs), :] = _dot(p, h_ref[...]).astype(BF16)

        hg = hc_ref[pl.ds(r0, gs), :]
        a = _dot(hg, wg_ref[...])
        a = a * _sigmoid(a) * _dot(hg, wu_ref[...])
        y = _dot(a.astype(BF16), wd_ref[...])

        @pl.when(f == 0)
        def _():
            yc_ref[pl.ds(r0, gs), :] = y

        @pl.when(f != 0)
        def _():
            yc_ref[pl.ds(r0, gs), :] += y

        @pl.when(f == nf - 1)
        def _():
            gg = jnp.sum(jnp.where(hit, gt, 0.0), axis=1, keepdims=True)
            hi, lo = _split2(yc_ref[pl.ds(r0, gs), :] * gg)
            o_ref[...] += _dot_tn(jnp.concatenate([p, p], axis=0), jnp.concatenate([hi, lo], axis=0))

        return 0

    lax.fori_loop(0, n_groups, group, 0)

    @pl.when((e == pl.num_programs(2) - 1) & (f == nf - 1))
    def _():
        o_ref[...] = x_ref[...] + gate_ref[...] * o_ref[...]


def _moe(x, g, sc, sh, gate, router, wg, wu, wd, tf):
    B, L, D = x.shape
    E, _, F = wg.shape
    tm = _row_tile(L, 1024)
    assert tm % MOE_GROUP == 0
    return pl.pallas_call(
        functools.partial(_moe_kernel, tm=tm),
        grid=(B, L // tm, E, F // tf),
        in_specs=[pl.BlockSpec((None, tm, D), lambda b, i, e, f: (b, i, 0)),
                  pl.BlockSpec((1, D), lambda b, i, e, f: (0, 0)),
                  _mod_spec(sc, tm), _mod_spec(sh, tm), _mod_spec(gate, tm),
                  pl.BlockSpec(router.shape, lambda b, i, e, f: (0, 0)),
                  pl.BlockSpec((None, D, tf), lambda b, i, e, f: (e, 0, f)),
                  pl.BlockSpec((None, D, tf), lambda b, i, e, f: (e, 0, f)),
                  pl.BlockSpec((None, tf, D), lambda b, i, e, f: (e, f, 0))],
        out_specs=pl.BlockSpec((None, tm, D), lambda b, i, e, f: (b, i, 0)),
        out_shape=jax.ShapeDtypeStruct((B, L, D), F32),
        scratch_shapes=[pltpu.VMEM((tm, D), BF16), pltpu.VMEM((tm, LANES), F32),
                        pltpu.VMEM((LANES, tm), F32), pltpu.VMEM((LANES, tm), F32),
                        pltpu.VMEM((tm, D), BF16), pltpu.VMEM((tm, D), F32)],
        compiler_params=_cparams(("arbitrary",) * 4),
        name="expert_swiglu",
    )(x, g.reshape(1, D), sc, sh, gate, router, wg, wu, wd)


_P1_Q = 0
_P1_K = 1024
_P1_V = 1280
_P1_QI = 1536
_P1_KD = 2048
_P1_VD = 2560
_P1_KIWI = 3072
_P1_KID = 3200
_P1_N = 3328


def _rope(a, c, s1, s2):
    return a * c + pltpu.roll(a, LANES - ROPE_DIM // 2, 1) * s1 + pltpu.roll(a, ROPE_DIM // 2, 1) * s2


def _proj1_kernel(x_ref, g_ref, sc_ref, sh_ref, w_ref, wvt_ref, c_ref, s1_ref, s2_ref, qg_ref, kg_ref,
                  q_ref, k_ref, v_ref, qi_ref, kd_ref, vd_ref, kiwi_ref, kid_ref, *, vt):
    h = _modnorm(x_ref[...], g_ref[...], sc_ref[...], sh_ref[...]).astype(BF16)
    cos, s1, s2 = c_ref[...], s1_ref[...], s2_ref[...]
    rr = lax.broadcasted_iota(I32, (LANES, LANES), 0) // HEAD_DIM
    cc = lax.broadcasted_iota(I32, (LANES, LANES), 1) // HEAD_DIM
    bd = jnp.where(rr == cc, 1.0, 0.0).astype(BF16)
    lane = lax.broadcasted_iota(I32, (1, LANES), 1)

    def headnorm(a, gain):
        hi, lo = _split2(a * a)
        ss = _dot(hi, bd) + _dot(lo, bd)
        return a * lax.rsqrt(ss * (1.0 / HEAD_DIM) + EPS) * gain

    def group(c0, width):
        y = _dot(h, w_ref[:, c0:c0 + width])
        return [y[:, i * LANES:(i + 1) * LANES] for i in range(width // LANES)]

    qscale = HEAD_DIM ** -0.5 * LOG2E
    for gidx in range(2):
        for i, a in enumerate(group(_P1_Q + gidx * 512, 512)):
            a = _rope(headnorm(a, qg_ref[...]), cos, s1, s2)
            cidx = gidx * 4 + i
            q_ref[:, cidx * LANES:(cidx + 1) * LANES] = (a * qscale).astype(BF16)
    kv = group(_P1_K, 512)
    for i in range(2):
        k_ref[:, i * LANES:(i + 1) * LANES] = _rope(headnorm(kv[i], kg_ref[...]), cos, s1, s2)
        v_ref[:, i * LANES:(i + 1) * LANES] = kv[2 + i]
    for i, a in enumerate(group(_P1_QI, 512)):
        qi_ref[:, i * LANES:(i + 1) * LANES] = (_rope(a, cos, s1, s2) * (HEAD_DIM ** -0.5)).astype(BF16)
    for i, a in enumerate(group(_P1_KD, 512)):
        kd_ref[:, i * LANES:(i + 1) * LANES] = _rope(headnorm(a, kg_ref[...]), cos, s1, s2).astype(BF16)
    if vt:
        vrow = lax.broadcasted_iota(I32, (wvt_ref.shape[0], 1), 0)
        vd_ref[...] = jnp.where(vrow % LANES < HEAD_DIM, _dot_nt(wvt_ref[...], h), 1.0).astype(BF16)
    else:
        for i, a in enumerate(group(_P1_VD, 512)):
            vd_ref[:, i * LANES:(i + 1) * LANES] = a.astype(BF16)
    kiwi, kid = group(_P1_KIWI, 256)
    first = lane < HEAD_DIM
    kiwi = _rope(kiwi, jnp.where(first, cos, 1.0), jnp.where(first, s1, 0.0), jnp.where(first, s2, 0.0))
    is_wi = (lane >= HEAD_DIM) & (lane < HEAD_DIM + IDX_HEADS)
    kiwi_ref[...] = kiwi * jnp.where(is_wi, IDX_HEADS ** -0.5, 1.0)
    kid_ref[...] = _rope(kid, cos, s1, s2).astype(BF16)


def _proj1(x, g, sc, sh, w, rope_tabs, qg, kg, vt):
    B, L, D = x.shape
    tm = _row_tile(L, 512)
    row = lambda d, dt: (pl.BlockSpec((None, tm, d), lambda b, i: (b, i, 0)), jax.ShapeDtypeStruct((B, L, d), dt))
    vd_out = row(512, BF16)
    if vt:
        vd_out = (pl.BlockSpec((None, 512, tm), lambda b, i: (b, 0, i)), jax.ShapeDtypeStruct((B, 512, L), BF16))
    outs = [row(1024, BF16), row(256, F32), row(256, F32), row(512, BF16), row(512, BF16), vd_out,
            row(LANES, F32), row(LANES, BF16)]
    wvt = w[:, _P1_VD:_P1_VD + 512].T
    tab = pl.BlockSpec((tm, LANES), lambda b, i: (i, 0))
    vec = pl.BlockSpec((1, LANES), lambda b, i: (0, 0))
    return pl.pallas_call(
        functools.partial(_proj1_kernel, vt=vt),
        grid=(B, L // tm),
        in_specs=[pl.BlockSpec((None, tm, D), lambda b, i: (b, i, 0)),
                  pl.BlockSpec((1, D), lambda b, i: (0, 0)),
                  _mod_spec(sc, tm), _mod_spec(sh, tm),
                  pl.BlockSpec(w.shape, lambda b, i: (0, 0)),
                  pl.BlockSpec(wvt.shape, lambda b, i: (0, 0)),
                  tab, tab, tab, vec, vec],
        out_specs=[o[0] for o in outs],
        out_shape=[o[1] for o in outs],
        compiler_params=_cparams(("arbitrary", "arbitrary")),
        name="l1_in_proj",
    )(x, g.reshape(1, D), sc, sh, w, wvt, *rope_tabs, qg, kg)


def _pad_l1_weight(w):
    D = w.shape[0]
    q, k, v, qi, ki, wi = jnp.split(w, [1024, 1280, 1536, 2048, 2112], axis=1)
    dup = lambda m, nh: jnp.repeat(m.reshape(D, nh, 1, HEAD_DIM), 2, axis=2).reshape(D, nh * 2 * HEAD_DIM)
    kiwi = jnp.concatenate([ki, wi, jnp.zeros((D, LANES - HEAD_DIM - IDX_HEADS), w.dtype)], axis=1)
    out = jnp.concatenate([q, k, v, qi, dup(k, DSA_KV_HEADS), dup(v, DSA_KV_HEADS), kiwi, dup(ki, 1)], axis=1)
    assert out.shape[1] == _P1_N
    return out.astype(BF16)


def _rope_tables(pos):
    half = ROPE_DIM // 2
    inv = 1.0 / (ROPE_THETA ** (jnp.arange(half, dtype=F32) / half))
    ang = pos.astype(F32)[:, None] * inv[None, :]
    cos, sin = jnp.cos(ang), jnp.sin(ang)
    n = pos.shape[0]
    one = jnp.ones((n, HEAD_DIM - ROPE_DIM), F32)
    zero = jnp.zeros((n, HEAD_DIM - ROPE_DIM), F32)
    z8 = jnp.zeros((n, half), F32)
    c = jnp.concatenate([cos, cos, one], axis=1)
    s1 = jnp.concatenate([-sin, z8, zero], axis=1)
    s2 = jnp.concatenate([z8, sin, zero], axis=1)
    return tuple(jnp.tile(t, (1, 2)) for t in (c, s1, s2))


def _sort_key(score):
    b = lax.bitcast_convert_type(score + 0.0, I32)
    return jnp.where(b < 0, b ^ 0x7FFFFFFF, b)


def _sort_key_const(value):
    b = int(np.float32(value).view(np.int32))
    return b ^ 0x7FFFFFFF if b < 0 else b


def _topk_kernel(qi_ref, kiwi_ref, kid_ref, mask_ref, key_ref, *, tq, tk, past, s_valid, topk, nkb_max, mask_t):
    i = pl.program_id(1)
    qbase = past + i * tq
    qrow = qbase + lax.broadcasted_iota(I32, (tq, 1), 0)
    adm_lim = jnp.minimum(((qrow >> 6) + 1) << 6, s_valid)
    lim = jnp.minimum((((qbase + tq - 1) >> 6) + 1) << 6, s_valid)
    nkb = jnp.minimum((lim + tk - 1) // tk, nkb_max)
    n_out = (jnp.zeros((tq, 1), I32) + (s_valid - jnp.minimum(nkb * tk, s_valid))).astype(F32)
    lane = lax.broadcasted_iota(I32, (1, LANES), 1)
    half = [lane < HEAD_DIM, lane >= HEAD_DIM]
    col = lax.broadcasted_iota(I32, (tq, tk), 1)
    negkey = _sort_key_const(NEG)
    kiwi = kiwi_ref[...]
    wi = [kiwi[:, HEAD_DIM + h:HEAD_DIM + h + 1] for h in range(IDX_HEADS)]
    qi = qi_ref[...]
    qim = []
    for h in range(IDX_HEADS):
        qc = qi[:, (h // 2) * LANES:(h // 2 + 1) * LANES]
        qim.append(jnp.where(half[h % 2], qc, jnp.zeros_like(qc)))

    def score_body(j, _):
        k0 = pl.multiple_of(j * tk, tk)
        kb = kid_ref[pl.ds(k0, tk), :]
        sc = jnp.zeros((tq, tk), F32)
        for h in range(IDX_HEADS):
            sc = sc + wi[h] * jnp.maximum(_dot_nt(qim[h], kb), 0.0)
        kpos = col + k0
        key = _sort_key(jnp.where(kpos < adm_lim, sc, NEG))
        key_ref[j] = jnp.where(kpos < s_valid, key, INT_MIN)
        return 0

    lax.fori_loop(0, nkb, score_body, 0)

    def count(thr, strict):
        def cbody(j, acc):
            kk = key_ref[j]
            for c in range(tk // LANES):
                kc = kk[:, c * LANES:(c + 1) * LANES]
                hit = (kc > thr) if strict else (kc >= thr)
                acc = acc + jnp.where(hit, 1.0, 0.0)
            return acc
        acc = lax.fori_loop(0, nkb, cbody, jnp.zeros((tq, LANES), F32))
        cnt = jnp.sum(acc, axis=-1, keepdims=True)
        out_hit = (negkey > thr) if strict else (negkey >= thr)
        return cnt + jnp.where(out_hit, n_out, 0.0)

    kf = float(topk)

    def bis_body(p, prefix):
        cand = prefix | jnp.left_shift(jnp.int32(1), 31 - p)
        cnt = count(cand ^ INT_MIN, False)
        return jnp.where(cnt >= kf, cand, prefix)

    prefix = lax.fori_loop(0, 32, bis_body, jnp.zeros((tq, 1), I32))
    tau = prefix ^ INT_MIN
    need = kf - count(tau, True)

    rr = lax.broadcasted_iota(I32, (tk, tk), 0)
    cc = lax.broadcasted_iota(I32, (tk, tk), 1)
    U = jnp.where(rr < cc, 1.0, 0.0).astype(BF16)

    def sel_body(j, carry):
        kk = key_ref[j]
        eq = jnp.where(kk == tau, 1.0, 0.0)
        rank = carry + _dot(eq.astype(BF16), U)
        take = jnp.where(kk > tau, 1.0, jnp.where(rank < need, eq, 0.0))
        kpos = col + j * tk
        mk = jnp.where(kpos < adm_lim, take, 0.0)
        mask_ref[j] = (mk.T if mask_t else mk).astype(mask_ref.dtype)
        return carry + jnp.sum(eq, axis=-1, keepdims=True)

    lax.fori_loop(0, nkb, sel_body, jnp.zeros((tq, 1), F32))

    def zero_body(j, _):
        mask_ref[j] = jnp.zeros(mask_ref.shape[1:], mask_ref.dtype)
        return 0

    lax.fori_loop(nkb, nkb_max, zero_body, 0)


def _topk_mask(qi, kiwi, kid, past, s_valid, tq, tk, mask_t):
    B, L, _ = qi.shape
    S = kid.shape[1]
    nb = S // tk
    topk = min(TOPK_MAX, s_valid // 4)
    kern = functools.partial(_topk_kernel, tq=tq, tk=tk, past=past, s_valid=s_valid, topk=topk, nkb_max=nb,
                             mask_t=mask_t)
    mshape = (tk, tq) if mask_t else (tq, tk)
    return pl.pallas_call(
        kern,
        grid=(B, L // tq),
        in_specs=[pl.BlockSpec((None, tq, qi.shape[-1]), lambda b, i: (b, i, 0)),
                  pl.BlockSpec((None, tq, LANES), lambda b, i: (b, i, 0)),
                  pl.BlockSpec((None, S, LANES), lambda b, i: (b, 0, 0))],
        out_specs=pl.BlockSpec((None, None, nb) + mshape, lambda b, i: (b, i, 0, 0, 0)),
        out_shape=jax.ShapeDtypeStruct((B, L // tq, nb) + mshape, BF16),
        scratch_shapes=[pltpu.VMEM((nb, tq, tk), I32)],
        compiler_params=_cparams(("arbitrary", "arbitrary")),
        name="indexer_topk",
    )(qi, kiwi, kid)


DSA_CHAINS = 2
DSA_T_CHAINS = 4


def _dsa_kernel(q_ref, kd_ref, vd_ref, mask_ref, o_ref, m_ref, l_ref, acc_ref, *, tq, tk, past, s_valid, nkb_max):
    i = pl.program_id(1)
    qbase = past + i * tq
    lim = jnp.minimum((((qbase + tq - 1) >> 6) + 1) << 6, s_valid)
    nkb = jnp.minimum((lim + tk - 1) // tk, nkb_max)
    lane = lax.broadcasted_iota(I32, (1, LANES), 1)
    first = lane < HEAD_DIM
    hpp = m_ref.shape[0]
    for j0 in range(0, DSA_KV_HEADS, hpp):
        q4s = []
        for j in range(j0, j0 + hpp):
            qs = []
            for c in (2 * j, 2 * j + 1):
                qc = q_ref[:, c * LANES:(c + 1) * LANES]
                qs += [jnp.where(first, qc, jnp.zeros_like(qc)), jnp.where(first, jnp.zeros_like(qc), qc)]
            q4s.append(jnp.concatenate(qs, axis=0))
        m_ref[...] = jnp.full(m_ref.shape, NEG, F32)
        l_ref[...] = jnp.zeros_like(l_ref)
        acc_ref[...] = jnp.zeros_like(acc_ref)

        def body(jb, _):
            k0 = pl.multiple_of(jb * tk, tk)
            mk = mask_ref[jb].astype(F32)
            sel4 = jnp.concatenate([mk] * 4, axis=0) > 0.5
            for jj in range(hpp):
                j = j0 + jj
                kb = kd_ref[pl.ds(k0, tk), j * LANES:(j + 1) * LANES]
                vb = vd_ref[pl.ds(k0, tk), j * LANES:(j + 1) * LANES]
                s = jnp.where(sel4, _dot_nt(q4s[jj], kb), -jnp.inf)
                m_old = m_ref[jj]
                m_new = jnp.maximum(m_old, jnp.max(s, axis=-1, keepdims=True))
                alpha = jnp.exp2(m_old - m_new)
                p = jnp.exp2(s - m_new)
                l_ref[jj] = alpha * l_ref[jj] + jnp.sum(p, axis=-1, keepdims=True)
                acc_ref[jj] = alpha * acc_ref[jj] + _dot(p.astype(BF16), vb)
                m_ref[jj] = m_new
            return 0

        lax.fori_loop(0, nkb, body, 0)
        for jj in range(hpp):
            o4 = acc_ref[jj] / l_ref[jj]
            for cc in range(2):
                oc = jnp.where(first, o4[(2 * cc) * tq:(2 * cc + 1) * tq], o4[(2 * cc + 1) * tq:(2 * cc + 2) * tq])
                c = 2 * (j0 + jj) + cc
                o_ref[:, c * LANES:(c + 1) * LANES] = oc.astype(o_ref.dtype)


def _dsa_t_kernel(q_ref, kd_ref, vt_ref, mask_ref, o_ref, acc_ref, *, tq, tk, past, s_valid, nkb_max):
    i = pl.program_id(1)
    qbase = past + i * tq
    lim = jnp.minimum((((qbase + tq - 1) >> 6) + 1) << 6, s_valid)
    nkb = jnp.minimum((lim + tk - 1) // tk, nkb_max)
    lane = lax.broadcasted_iota(I32, (1, LANES), 1)
    first = lane < HEAD_DIM
    nch = acc_ref.shape[0]
    pv_rows = HEAD_DIM + 16
    for j0 in range(0, DSA_KV_HEADS, nch):
        q4s = []
        for j in range(j0, j0 + nch):
            qs = []
            for c in (2 * j, 2 * j + 1):
                qc = q_ref[:, c * LANES:(c + 1) * LANES]
                qs += [jnp.where(first, qc, jnp.zeros_like(qc)), jnp.where(first, jnp.zeros_like(qc), qc)]
            q4s.append(jnp.concatenate(qs, axis=0))
        acc_ref[...] = jnp.zeros_like(acc_ref)

        def body(jb, carry):
            k0 = pl.multiple_of(jb * tk, tk)
            nsub = tk // mask_ref.shape[1]
            mk = jnp.concatenate([mask_ref[jb * nsub + t] for t in range(nsub)], axis=0).astype(F32)
            sel4 = jnp.concatenate([mk] * 4, axis=1) > 0.5
            s_raw, p_bf, m_new, alpha = {}, {}, {}, {}

            def qk(jj):
                j = j0 + jj
                kb = kd_ref[pl.ds(k0, tk), j * LANES:(j + 1) * LANES]
                s_raw[jj] = _dot_nt(kb, q4s[jj])

            def softmax(jj):
                s = jnp.where(sel4, s_raw.pop(jj), -jnp.inf)
                m_new[jj] = jnp.maximum(carry[jj], jnp.max(s, axis=0, keepdims=True))
                alpha[jj] = jnp.exp2(carry[jj] - m_new[jj])
                p_bf[jj] = jnp.exp2(s - m_new[jj]).astype(BF16)

            def pv(jj):
                j = j0 + jj
                vb = vt_ref[j * LANES:j * LANES + pv_rows, pl.ds(k0, tk)]
                acc_ref[jj, 0:pv_rows, :] = alpha[jj] * acc_ref[jj, 0:pv_rows, :] + _dot(vb, p_bf.pop(jj))

            qk(0)
            for jj in range(nch):
                if jj + 1 < nch:
                    qk(jj + 1)
                softmax(jj)
                if jj > 0:
                    pv(jj - 1)
            pv(nch - 1)
            return tuple(m_new[jj] for jj in range(nch))

        ms = lax.fori_loop(0, nkb, body, (jnp.full((1, 4 * tq), NEG, F32),) * nch)
        del ms
        for jj in range(nch):
            acc = acc_ref[jj]
            o4 = acc / acc[HEAD_DIM:HEAD_DIM + 1, :]
            heads = [o4[:, g * tq:(g + 1) * tq].T for g in range(4)]
            for cc in range(2):
                c = 2 * (j0 + jj) + cc
                pair = jnp.where(first, heads[2 * cc], pltpu.roll(heads[2 * cc + 1], HEAD_DIM, 1))
                o_ref[:, c * LANES:(c + 1) * LANES] = pair.astype(o_ref.dtype)


def _dsa_attention_t(q, kd, vt, mask_t, past, s_valid, tq, tk):
    B, L, W = q.shape
    S = kd.shape[1]
    nb = S // tk
    assert tq == LANES and tk % mask_t.shape[3] == 0 and S % tk == 0
    kern = functools.partial(_dsa_t_kernel, tq=tq, tk=tk, past=past, s_valid=s_valid, nkb_max=nb)
    return pl.pallas_call(
        kern,
        grid=(B, L // tq),
        in_specs=[pl.BlockSpec((None, tq, W), lambda b, i: (b, i, 0)),
                  pl.BlockSpec((None, S, kd.shape[-1]), lambda b, i: (b, 0, 0)),
                  pl.BlockSpec((None, vt.shape[1], S), lambda b, i: (b, 0, 0)),
                  pl.BlockSpec((None, None) + mask_t.shape[2:], lambda b, i: (b, i, 0, 0, 0))],
        out_specs=pl.BlockSpec((None, tq, W), lambda b, i: (b, i, 0)),
        out_shape=jax.ShapeDtypeStruct((B, L, W), BF16),
        scratch_shapes=[pltpu.VMEM((DSA_T_CHAINS, LANES, 4 * tq), F32)],
        compiler_params=_cparams(("arbitrary", "arbitrary")),
        name="sparse_attn_t",
    )(q, kd, vt, mask_t)


def _dsa_attention(q, kd, vd, mask, past, s_valid, tq, tk):
    B, L, W = q.shape
    S = kd.shape[1]
    nb = S // tk
    kern = functools.partial(_dsa_kernel, tq=tq, tk=tk, past=past, s_valid=s_valid, nkb_max=nb)
    return pl.pallas_call(
        kern,
        grid=(B, L // tq),
        in_specs=[pl.BlockSpec((None, tq, W), lambda b, i: (b, i, 0)),
                  pl.BlockSpec((None, S, kd.shape[-1]), lambda b, i: (b, 0, 0)),
                  pl.BlockSpec((None, S, vd.shape[-1]), lambda b, i: (b, 0, 0)),
                  pl.BlockSpec((None, None, nb, tq, tk), lambda b, i: (b, i, 0, 0, 0))],
        out_specs=pl.BlockSpec((None, tq, W), lambda b, i: (b, i, 0)),
        out_shape=jax.ShapeDtypeStruct((B, L, W), BF16),
        scratch_shapes=[pltpu.VMEM((DSA_CHAINS, 4 * tq, 1), F32), pltpu.VMEM((DSA_CHAINS, 4 * tq, 1), F32),
                        pltpu.VMEM((DSA_CHAINS, 4 * tq, LANES), F32)],
        compiler_params=_cparams(("arbitrary", "arbitrary")),
        name="sparse_attn",
    )(q, kd, vd, mask)


def _pad_rows(a, mult):
    s = a.shape[1]
    sp = -(-s // mult) * mult
    if sp == s:
        return a
    return jnp.concatenate([a, jnp.zeros((a.shape[0], sp - s) + a.shape[2:], a.dtype)], axis=1)


def _dup_heads(a):
    B, S, H, d = a.shape
    return jnp.repeat(a[:, :, :, None, :], 2, axis=3).reshape(B, S, H * 2 * d)


SB_TQ, SB_TK = 512, 256
SB_TK_DECODE = 512
IDX_TK = 512
DSA_TK = 1024


def _trunk(x, mods0, mods1, past, W, flat):
    B, L, D = x.shape
    past_len = 0 if past is None else past[0].shape[1]

    def tok(a):
        return a.reshape(1, B * L, a.shape[-1]) if flat else a

    def untok(a):
        return a.reshape(B, L, a.shape[-1]) if flat else a

    def mod(m):
        if flat:
            return jnp.repeat(m, L, axis=0).reshape(1, B * L, D)
        return m[:, None, :]

    sh_m, sc_m, g_m, sh_f, sc_f, g_f = [mod(m) for m in mods0]
    xt = tok(x)

    q, k, v, kb, vb, u = _proj0(xt, W['l0_norm_mix'], sc_m, sh_m, W['l0_w_in'])
    q, k, v, kb, vb, u = [untok(a) for a in (q, k, v, kb, vb, u)]
    if past is None:
        tq, tk = _row_tile(L, SB_TQ), min(L, SB_TK)
        k_all, v_all = kb, vb
        prev = jnp.zeros((B, 32, u.shape[-1]), F32)
    else:
        tq, tk = L, SB_TK_DECODE
        k_all = _pad_rows(jnp.concatenate([past[0].reshape(B, past_len, -1).astype(BF16), kb], axis=1), tk)
        v_all = _pad_rows(jnp.concatenate([past[1].reshape(B, past_len, -1).astype(BF16), vb], axis=1), tk)
        prev = jnp.concatenate([jnp.zeros((B, 2, u.shape[-1]), F32), past[2]], axis=1)
    o_a = _sb_attention(q, k_all, v_all, past_len, tq, tk)
    o_b = _conv_module(u, prev, W['l0_conv_w'], W['l0_conv_b'], W['l0_conv_ln_g'], W['l0_conv_ln_b'])
    conv_state = jnp.concatenate([prev[:, 2:], u], axis=1)[:, -(CONV_WIDTH - 1):]
    xt = _outproj(xt, g_m, [tok(o_a), tok(o_b)], [W['l0_w_out'][:512], W['l0_w_out'][512:]])
    xt = _ffn(xt, W['l0_norm_ff'], sc_f, sh_f, g_f, W['l0_ff_wg'], W['l0_ff_wu'], W['l0_ff_wd'], 1408)

    sh_m, sc_m, g_m, sh_f, sc_f, g_f = [mod(m) for m in mods1]
    pos = past_len + jnp.arange(L, dtype=I32)
    tabs = _rope_tables(pos)
    if flat:
        tabs = tuple(jnp.tile(t, (B, 1)) for t in tabs)
    qg = jnp.tile(W['l1_q_norm'].reshape(1, HEAD_DIM), (1, 2))
    kg = jnp.tile(W['l1_k_norm'].reshape(1, HEAD_DIM), (1, 2))
    keys_on_sublanes = past is None and L % LANES == 0
    q1, k1, v1, qi, kd, vd, kiwi, kid = _proj1(xt, W['l1_norm_mix'], sc_m, sh_m, W['l1_w_in'], tabs, qg, kg,
                                               vt=keys_on_sublanes)
    q1, k1, v1, qi, kd, kiwi, kid = [untok(a) for a in (q1, k1, v1, qi, kd, kiwi, kid)]
    if not keys_on_sublanes:
        vd = untok(vd)
    s_valid = past_len + L
    tq1 = min(L, LANES)
    tk1 = min(L, IDX_TK) if past is None else IDX_TK
    if past is not None:
        kd = _pad_rows(jnp.concatenate([_dup_heads(past[3]).astype(BF16), kd], axis=1), tk1)
        vd = _pad_rows(jnp.concatenate([_dup_heads(past[4]).astype(BF16), vd], axis=1), tk1)
        kid = _pad_rows(jnp.concatenate([_dup_heads(past[5][:, :, None, :]).astype(BF16), kid], axis=1), tk1)
    mask = _topk_mask(qi, kiwi, kid, past_len, s_valid, tq1, tk1, mask_t=keys_on_sublanes)
    if keys_on_sublanes:
        tk_attn = DSA_TK if kd.shape[1] % DSA_TK == 0 else tk1
        o1 = _dsa_attention_t(q1, kd, vd, mask, past_len, s_valid, tq1, tk_attn)
    else:
        o1 = _dsa_attention(q1, kd, vd, mask, past_len, s_valid, tq1, tk1)
    xt = _outproj(xt, g_m, [tok(o1)], [W['l1_w_out']])
    xt = _moe(xt, W['l1_norm_ff'], sc_f, sh_f, g_f, W['l1_router'], W['l1_exp_wg'], W['l1_exp_wu'],
              W['l1_exp_wd'], 1792)

    hd = HEAD_DIM
    states = (k.reshape(B, L, SB_HEADS, hd), v.reshape(B, L, SB_HEADS, hd), conv_state,
              k1.reshape(B, L, DSA_KV_HEADS, hd), v1.reshape(B, L, DSA_KV_HEADS, hd), kiwi[..., :hd])
    return untok(xt), states


def kernel(x_prompt, x_sample, c_prompt, c_sample, cache_sb_k, cache_sb_v, cache_conv, cache_dsa_k, cache_dsa_v, cache_dsa_kidx, l0_ada_w, l0_ada_b, l0_norm_mix, l0_w_in, l0_conv_w, l0_conv_b, l0_conv_ln_g, l0_conv_ln_b, l0_w_out, l0_norm_ff, l0_ff_wg, l0_ff_wu, l0_ff_wd, l1_ada_w, l1_ada_b, l1_norm_mix, l1_w_in, l1_q_norm, l1_k_norm, l1_w_out, l1_norm_ff, l1_router, l1_exp_wg, l1_exp_wu, l1_exp_wd):
    D = x_prompt.shape[-1]
    router = jnp.concatenate([l1_router, jnp.zeros((D, LANES - N_EXPERTS), F32)], axis=1).astype(BF16)
    W = dict(l0_norm_mix=l0_norm_mix, l0_w_in=l0_w_in.astype(BF16), l0_conv_w=l0_conv_w, l0_conv_b=l0_conv_b,
             l0_conv_ln_g=l0_conv_ln_g, l0_conv_ln_b=l0_conv_ln_b, l0_w_out=l0_w_out.astype(BF16),
             l0_norm_ff=l0_norm_ff, l0_ff_wg=l0_ff_wg.astype(BF16), l0_ff_wu=l0_ff_wu.astype(BF16),
             l0_ff_wd=l0_ff_wd.astype(BF16), l1_norm_mix=l1_norm_mix, l1_w_in=_pad_l1_weight(l1_w_in),
             l1_q_norm=l1_q_norm, l1_k_norm=l1_k_norm, l1_w_out=l1_w_out.astype(BF16), l1_norm_ff=l1_norm_ff,
             l1_router=router, l1_exp_wg=l1_exp_wg.astype(BF16), l1_exp_wu=l1_exp_wu.astype(BF16),
             l1_exp_wd=l1_exp_wd.astype(BF16))
    bp = c_prompt.shape[0]
    c_all = jnp.concatenate([c_prompt, c_sample], axis=0)
    m0 = _ada(c_all, l0_ada_w, l0_ada_b)
    m1 = _ada(c_all, l1_ada_w, l1_ada_b)
    mods = lambda m, sl: [t[sl] for t in jnp.split(m, 6, axis=-1)]
    y_p, st_p = _trunk(x_prompt, mods(m0, slice(0, bp)), mods(m1, slice(0, bp)), None, W, flat=False)
    past = (cache_sb_k, cache_sb_v, cache_conv, cache_dsa_k, cache_dsa_v, cache_dsa_kidx)
    y_s, st_s = _trunk(x_sample, mods(m0, slice(bp, None)), mods(m1, slice(bp, None)), past, W, flat=True)
    return (y_p, y_s) + tuple(st_p) + tuple(st_s)
```

```python
import functools
import math

import numpy as np
import jax
import jax.numpy as jnp
from jax import lax
from jax.experimental import pallas as pl
from jax.experimental.pallas import tpu as pltpu

F32 = jnp.float32
BF16 = jnp.bfloat16
I32 = jnp.int32

EPS = 1e-6
NEG = -1e30
HEAD_DIM = 64
CHUNK = 64
TOPK_MAX = 256
ROPE_DIM = 16
ROPE_THETA = 500000.0
CONV_WIDTH = 31
SB_HEADS = 8
DSA_HEADS = 16
DSA_KV_HEADS = 4
IDX_HEADS = 8
N_EXPERTS = 8
LANES = 128
INT_MIN = -2 ** 31
LOG2E = math.log2(math.e)
VMEM_LIMIT = 56 * 2 ** 20


def _cparams(sem):
    return pltpu.CompilerParams(dimension_semantics=sem, vmem_limit_bytes=VMEM_LIMIT)


def _dot(a, b):
    return jnp.dot(a, b, preferred_element_type=F32)


def _dot_nt(a, b):
    return lax.dot_general(a, b, (((1,), (1,)), ((), ())), preferred_element_type=F32)


def _split2(x):
    hi = x.astype(BF16)
    lo = (x - hi.astype(F32)).astype(BF16)
    return hi, lo


def _sigmoid(x):
    return 1.0 / (1.0 + jnp.exp(-x))


def _modnorm(x, g, sc, sh):
    ms = jnp.mean(x * x, axis=-1, keepdims=True)
    y = x * lax.rsqrt(ms + EPS) * g
    return y * (1.0 + sc) + sh


def _mod_spec(mod, tm):
    if mod.shape[1] == 1:
        return pl.BlockSpec((None, 1, mod.shape[2]), lambda b, i, *_: (b, 0, 0))
    return pl.BlockSpec((None, tm, mod.shape[2]), lambda b, i, *_: (b, i, 0))


def _row_tile(L, pref):
    return pref if L % pref == 0 else L


def _ada_kernel(c_ref, w_ref, b_ref, o_ref):
    c = c_ref[...]
    s = c * _sigmoid(c)
    s_hi, s_lo = _split2(s)
    w_hi, w_lo = _split2(w_ref[...])
    o_ref[...] = _dot(s_hi, w_hi) + _dot(s_lo, w_hi) + _dot(s_hi, w_lo) + b_ref[...]


def _ada(c, w, b):
    bc, d = c.shape
    n = w.shape[1]
    tn = 512
    return pl.pallas_call(
        _ada_kernel,
        grid=(n // tn,),
        in_specs=[pl.BlockSpec((bc, d), lambda j: (0, 0)),
                  pl.BlockSpec((d, tn), lambda j: (0, j)),
                  pl.BlockSpec((1, tn), lambda j: (0, j))],
        out_specs=pl.BlockSpec((bc, tn), lambda j: (0, j)),
        out_shape=jax.ShapeDtypeStruct((bc, n), F32),
        compiler_params=_cparams(("arbitrary",)),
        name="ada_mod",
    )(c, w, b.reshape(1, n))


def _proj0_kernel(x_ref, g_ref, sc_ref, sh_ref, w_ref, q_ref, k_ref, v_ref, kb_ref, vb_ref, u_ref):
    h = _modnorm(x_ref[...], g_ref[...], sc_ref[...], sh_ref[...]).astype(BF16)
    wd = q_ref.shape[-1]

    def mm(c):
        return _dot(h, w_ref[:, c * wd:(c + 1) * wd])

    q_ref[...] = (mm(0) * (HEAD_DIM ** -0.5)).astype(BF16)
    k = mm(1)
    k_ref[...] = k
    kb_ref[...] = k.astype(BF16)
    v = mm(2)
    v_ref[...] = v
    vb_ref[...] = v.astype(BF16)
    u_ref[...] = mm(3) * _sigmoid(mm(4))


def _proj0(x, g, sc, sh, w):
    B, L, D = x.shape
    wd = SB_HEADS * HEAD_DIM
    tm = _row_tile(L, 512)
    row = lambda d, dt: (pl.BlockSpec((None, tm, d), lambda b, i: (b, i, 0)), jax.ShapeDtypeStruct((B, L, d), dt))
    outs = [row(wd, BF16), row(wd, F32), row(wd, F32), row(wd, BF16), row(wd, BF16), row(wd, F32)]
    return pl.pallas_call(
        _proj0_kernel,
        grid=(B, L // tm),
        in_specs=[pl.BlockSpec((None, tm, D), lambda b, i: (b, i, 0)),
                  pl.BlockSpec((1, D), lambda b, i: (0, 0)),
                  _mod_spec(sc, tm), _mod_spec(sh, tm),
                  pl.BlockSpec(w.shape, lambda b, i: (0, 0))],
        out_specs=[o[0] for o in outs],
        out_shape=[o[1] for o in outs],
        compiler_params=_cparams(("arbitrary", "arbitrary")),
        name="l0_in_proj",
    )(x, g.reshape(1, D), sc, sh, w)


SB_EXP_UNDERFLOW = -120.0


def _sb_kernel(q_ref, k_ref, v_ref, o_ref, *, tq, tk, past, nkb_max):
    i = pl.program_id(2)
    qbase = past + i * tq
    nkb = jnp.minimum((qbase + tq - 2 + tk) // tk, nkb_max)
    lane = lax.broadcasted_iota(I32, (1, LANES), 1)
    half = [lane < HEAD_DIM, lane >= HEAD_DIM]
    q = q_ref[...]
    qm = [jnp.where(half[hh], q, jnp.zeros_like(q)) for hh in range(2)]
    rr = lax.broadcasted_iota(I32, (tk, tk), 0)
    cc = lax.broadcasted_iota(I32, (tk, tk), 1)
    U = jnp.where(rr > cc, 1.0, 0.0).astype(BF16)
    dcol = lax.broadcasted_iota(I32, (tq, tk), 1) - lax.broadcasted_iota(I32, (tq, tk), 0)

    def make_body(masked):
        def body(jj, carry):
            r0, r1, acc = carry
            j = nkb - 1 - jj
            k0 = pl.multiple_of(j * tk, tk)
            kb = k_ref[pl.ds(k0, tk), :]
            vb = v_ref[pl.ds(k0, tk), :]
            mask = dcol < (qbase - k0)
            rs = [r0, r1]
            for hh in range(2):
                z = _dot_nt(qm[hh], kb)
                sp = jnp.log(1.0 + jnp.exp(-jnp.abs(z)))
                lb = jnp.minimum(z, 0.0) - sp
                l1m = -jnp.maximum(z, 0.0) - sp
                if masked:
                    l1m = jnp.where(mask, l1m, 0.0)
                hi, lo = _split2(l1m)
                cs = _dot(hi, U) + _dot(lo, U)
                w = jnp.exp(lb + cs + rs[hh])
                if masked:
                    w = jnp.where(mask, w, 0.0)
                vm = jnp.where(half[hh], vb, jnp.zeros_like(vb))
                acc = acc + _dot(w.astype(BF16), vm)
                rs[hh] = rs[hh] + cs[:, 0:1] + l1m[:, 0:1]
            return rs[0], rs[1], acc
        return body

    n_full = jnp.minimum(qbase // tk, nkb)
    z1 = jnp.zeros((tq, 1), F32)
    carry = lax.fori_loop(0, nkb - n_full, make_body(True), (z1, z1, jnp.zeros((tq, LANES), F32)))
    unmasked = make_body(False)

    def live(r0, r1):
        return jnp.max(jnp.maximum(r0, r1)) > SB_EXP_UNDERFLOW

    def w_cond(c):
        return (c[0] < nkb) & c[1]

    def w_body(c):
        r0, r1, acc = unmasked(c[0], c[2:])
        return c[0] + 1, live(r0, r1), r0, r1, acc

    out = lax.while_loop(w_cond, w_body, (nkb - n_full, live(carry[0], carry[1])) + tuple(carry))
    o_ref[...] = out[4].astype(o_ref.dtype)


def _sb_attention(q, k, v, past, tq, tk):
    B, L, W = q.shape
    S = k.shape[1]
    hp = W // LANES
    kern = functools.partial(_sb_kernel, tq=tq, tk=tk, past=past, nkb_max=S // tk)
    return pl.pallas_call(
        kern,
        grid=(B, hp, L // tq),
        in_specs=[pl.BlockSpec((None, tq, LANES), lambda b, h, i: (b, i, h)),
                  pl.BlockSpec((None, S, LANES), lambda b, h, i: (b, 0, h)),
                  pl.BlockSpec((None, S, LANES), lambda b, h, i: (b, 0, h))],
        out_specs=pl.BlockSpec((None, tq, LANES), lambda b, h, i: (b, i, h)),
        out_shape=jax.ShapeDtypeStruct((B, L, W), BF16),
        compiler_params=_cparams(("arbitrary", "arbitrary", "arbitrary")),
        name="stickbreak_attn",
    )(q, k, v)


def _conv_kernel(*refs, tm, has_halo):
    if has_halo:
        u_ref, halo_ref, prev_ref, w_ref, b_ref, g_ref, be_ref, o_ref, buf_ref, cv_ref = refs
    else:
        u_ref, prev_ref, w_ref, b_ref, g_ref, be_ref, o_ref, buf_ref, cv_ref = refs
    i = pl.program_id(1)
    pad = 32
    if has_halo:
        buf_ref[0:pad, :] = jnp.where(i == 0, prev_ref[...], halo_ref[...])
    else:
        buf_ref[0:pad, :] = prev_ref[...]
    buf_ref[pad:pad + tm, :] = u_ref[...]
    C = u_ref.shape[-1]
    rt = min(tm, 64)
    off = pad - (CONV_WIDTH - 1)
    for c in range(C // LANES):
        cs = slice(c * LANES, (c + 1) * LANES)
        for r in range(tm // rt):
            acc = jnp.zeros((rt, LANES), F32)
            for j in range(CONV_WIDTH):
                acc = acc + buf_ref[r * rt + off + j: r * rt + off + j + rt, cs] * w_ref[j:j + 1, cs]
            cv_ref[r * rt:(r + 1) * rt, cs] = acc
    conv = cv_ref[...] + b_ref[...]
    mu = jnp.mean(conv, axis=-1, keepdims=True)
    xc = conv - mu
    var = jnp.mean(xc * xc, axis=-1, keepdims=True)
    y = xc * lax.rsqrt(var + EPS) * g_ref[...] + be_ref[...]
    o_ref[...] = (y * _sigmoid(y)).astype(o_ref.dtype)


def _conv_module(u, prev32, w, b, g, be):
    B, L, C = u.shape
    tm = _row_tile(L, 256)
    has_halo = L > tm
    kern = functools.partial(_conv_kernel, tm=tm, has_halo=has_halo)
    in_specs = [pl.BlockSpec((None, tm, C), lambda b_, i: (b_, i, 0))]
    args = [u]
    if has_halo:
        r = tm // 32
        in_specs.append(pl.BlockSpec((None, 32, C), lambda b_, i: (b_, jnp.maximum(i * r - 1, 0), 0)))
        args.append(u)
    vec = pl.BlockSpec((1, C), lambda b_, i: (0, 0))
    in_specs += [pl.BlockSpec((None, 32, C), lambda b_, i: (b_, 0, 0)),
                 pl.BlockSpec((32, C), lambda b_, i: (0, 0)), vec, vec, vec]
    wpad = jnp.concatenate([w, jnp.zeros((32 - CONV_WIDTH, C), F32)], axis=0)
    args += [prev32, wpad, b.reshape(1, C), g.reshape(1, C), be.reshape(1, C)]
    return pl.pallas_call(
        kern,
        grid=(B, L // tm),
        in_specs=in_specs,
        out_specs=pl.BlockSpec((None, tm, C), lambda b_, i: (b_, i, 0)),
        out_shape=jax.ShapeDtypeStruct((B, L, C), BF16),
        scratch_shapes=[pltpu.VMEM((tm + 32, C), F32), pltpu.VMEM((tm, C), F32)],
        compiler_params=_cparams(("arbitrary", "arbitrary")),
        name="conv_module",
    )(*args)


def _outproj_kernel(*refs, n_in):
    x_ref, g_ref = refs[0], refs[1]
    a_refs = refs[2:2 + n_in]
    w_refs = refs[2 + n_in:2 + 2 * n_in]
    o_ref = refs[2 + 2 * n_in]
    acc = _dot(a_refs[0][...], w_refs[0][...])
    for a_ref, w_ref in zip(a_refs[1:], w_refs[1:]):
        acc = acc + _dot(a_ref[...], w_ref[...])
    o_ref[...] = x_ref[...] + g_ref[...] * acc


def _outproj(x, gate, acts, ws):
    B, L, D = x.shape
    tm = _row_tile(L, 512)
    n_in = len(acts)
    in_specs = [pl.BlockSpec((None, tm, D), lambda b, i: (b, i, 0)), _mod_spec(gate, tm)]
    in_specs += [pl.BlockSpec((None, tm, a.shape[-1]), lambda b, i: (b, i, 0)) for a in acts]
    in_specs += [pl.BlockSpec(w.shape, lambda b, i: (0, 0)) for w in ws]
    return pl.pallas_call(
        functools.partial(_outproj_kernel, n_in=n_in),
        grid=(B, L // tm),
        in_specs=in_specs,
        out_specs=pl.BlockSpec((None, tm, D), lambda b, i: (b, i, 0)),
        out_shape=jax.ShapeDtypeStruct((B, L, D), F32),
        compiler_params=_cparams(("arbitrary", "arbitrary")),
        name="out_proj",
    )(x, gate, *acts, *ws)


def _ffn_kernel(x_ref, g_ref, sc_ref, sh_ref, gate_ref, wg_ref, wu_ref, wd_ref, o_ref, h_ref, acc_ref):
    f = pl.program_id(2)

    @pl.when(f == 0)
    def _():
        h_ref[...] = _modnorm(x_ref[...], g_ref[...], sc_ref[...], sh_ref[...]).astype(BF16)
        acc_ref[...] = jnp.zeros_like(acc_ref)

    h = h_ref[...]
    a = _dot(h, wg_ref[...])
    a = a * _sigmoid(a) * _dot(h, wu_ref[...])
    acc_ref[...] += _dot(a.astype(BF16), wd_ref[...])

    @pl.when(f == pl.num_programs(2) - 1)
    def _():
        o_ref[...] = x_ref[...] + gate_ref[...] * acc_ref[...]


def _ffn(x, g, sc, sh, gate, wg, wu, wd, tf):
    B, L, D = x.shape
    F = wg.shape[1]
    tm = _row_tile(L, 512)
    return pl.pallas_call(
        _ffn_kernel,
        grid=(B, L // tm, F // tf),
        in_specs=[pl.BlockSpec((None, tm, D), lambda b, i, f: (b, i, 0)),
                  pl.BlockSpec((1, D), lambda b, i, f: (0, 0)),
                  _mod_spec(sc, tm), _mod_spec(sh, tm), _mod_spec(gate, tm),
                  pl.BlockSpec((D, tf), lambda b, i, f: (0, f)),
                  pl.BlockSpec((D, tf), lambda b, i, f: (0, f)),
                  pl.BlockSpec((tf, D), lambda b, i, f: (f, 0))],
        out_specs=pl.BlockSpec((None, tm, D), lambda b, i, f: (b, i, 0)),
        out_shape=jax.ShapeDtypeStruct((B, L, D), F32),
        scratch_shapes=[pltpu.VMEM((tm, D), BF16), pltpu.VMEM((tm, D), F32)],
        compiler_params=_cparams(("arbitrary", "arbitrary", "arbitrary")),
        name="dense_swiglu",
    )(x, g.reshape(1, D), sc, sh, gate, wg, wu, wd)


def _dot_tn(a, b):
    return lax.dot_general(a, b, (((0,), (0,)), ((), ())), preferred_element_type=F32)


MOE_GROUP = 128
RANK_BLOCK = 256


def _moe_kernel(x_ref, g_ref, sc_ref, sh_ref, gate_ref, r_ref, wg_ref, wu_ref, wd_ref, o_ref,
                h_ref, rk_ref, rankT_ref, dgT_ref, hc_ref, yc_ref, *, tm):
    e = pl.program_id(2)
    f = pl.program_id(3)
    nf = pl.num_programs(3)
    gs = MOE_GROUP
    lane = lax.broadcasted_iota(I32, (1, LANES), 1)

    @pl.when((e == 0) & (f == 0))
    def _():
        h = _modnorm(x_ref[...], g_ref[...], sc_ref[...], sh_ref[...]).astype(BF16)
        h_ref[...] = h
        o_ref[...] = jnp.zeros_like(o_ref)
        logits = jnp.where(lane < N_EXPERTS, _dot(h, r_ref[...]), -jnp.inf)
        m1 = jnp.max(logits, axis=-1, keepdims=True)
        i1 = jnp.min(jnp.where(logits == m1, lane, LANES), axis=-1, keepdims=True)
        rest = jnp.where(lane == i1, -jnp.inf, logits)
        m2 = jnp.max(rest, axis=-1, keepdims=True)
        i2 = jnp.min(jnp.where(rest == m2, lane, LANES), axis=-1, keepdims=True)
        e2 = jnp.exp(m2 - m1)
        den = 1.0 + e2
        dg = jnp.where(lane == i1, 1.0 / den, 0.0) + jnp.where(lane == i2, e2 / den, 0.0)
        sel = jnp.where(lane == i1, 1.0, 0.0) + jnp.where(lane == i2, 1.0, 0.0)
        rb = min(RANK_BLOCK, tm)
        rr = lax.broadcasted_iota(I32, (rb, rb), 0)
        cc = lax.broadcasted_iota(I32, (rb, rb), 1)
        lower = jnp.where(cc < rr, 1.0, 0.0).astype(BF16)
        carry = jnp.zeros((1, LANES), F32)
        for b in range(tm // rb):
            sb = sel[b * rb:(b + 1) * rb]
            rk_ref[b * rb:(b + 1) * rb, :] = carry + _dot(lower, sb.astype(BF16))
            carry = carry + jnp.sum(sb, axis=0, keepdims=True)
        rank = jnp.where(sel > 0.5, rk_ref[...], -1.0)
        rankT_ref[...] = rank.T
        dgT_ref[...] = dg.T

    rk = rankT_ref[pl.ds(e, 1), :]
    gt = dgT_ref[pl.ds(e, 1), :]
    cnt = jnp.sum(jnp.where(rk >= 0.0, 1.0, 0.0), axis=1, keepdims=True).astype(I32)[0, 0]
    n_groups = (cnt + gs - 1) // gs
    row = lax.broadcasted_iota(I32, (gs, 1), 0)

    def group(gi, _):
        r0 = pl.multiple_of(gi * gs, gs)
        hit = rk == (row + r0).astype(F32)
        p = jnp.where(hit, 1.0, 0.0).astype(BF16)

        @pl.when(f == 0)
        def _():
            hc_ref[pl.ds(r0, gs), :] = _dot(p, h_ref[...]).astype(BF16)

        hg = hc_ref[pl.ds(r0, gs), :]
        a = _dot(hg, wg_ref[...])
        a = a * _sigmoid(a) * _dot(hg, wu_ref[...])
        y = _dot(a.astype(BF16), wd_ref[...])

        @pl.when(f == 0)
        def _():
            yc_ref[pl.ds(r0, gs), :] = y

        @pl.when(f != 0)
        def _():
            yc_ref[pl.ds(r0, gs), :] += y

        @pl.when(f == nf - 1)
        def _():
            gg = jnp.sum(jnp.where(hit, gt, 0.0), axis=1, keepdims=True)
            hi, lo = _split2(yc_ref[pl.ds(r0, gs), :] * gg)
            o_ref[...] += _dot_tn(jnp.concatenate([p, p], axis=0), jnp.concatenate([hi, lo], axis=0))

        return 0

    lax.fori_loop(0, n_groups, group, 0)

    @pl.when((e == pl.num_programs(2) - 1) & (f == nf - 1))
    def _():
        o_ref[...] = x_ref[...] + gate_ref[...] * o_ref[...]


def _moe(x, g, sc, sh, gate, router, wg, wu, wd, tf):
    B, L, D = x.shape
    E, _, F = wg.shape
    tm = _row_tile(L, 1024)
    assert tm % MOE_GROUP == 0
    return pl.pallas_call(
        functools.partial(_moe_kernel, tm=tm),
        grid=(B, L // tm, E, F // tf),
        in_specs=[pl.BlockSpec((None, tm, D), lambda b, i, e, f: (b, i, 0)),
                  pl.BlockSpec((1, D), lambda b, i, e, f: (0, 0)),
                  _mod_spec(sc, tm), _mod_spec(sh, tm), _mod_spec(gate, tm),
                  pl.BlockSpec(router.shape, lambda b, i, e, f: (0, 0)),
                  pl.BlockSpec((None, D, tf), lambda b, i, e, f: (e, 0, f)),
                  pl.BlockSpec((None, D, tf), lambda b, i, e, f: (e, 0, f)),
                  pl.BlockSpec((None, tf, D), lambda b, i, e, f: (e, f, 0))],
        out_specs=pl.BlockSpec((None, tm, D), lambda b, i, e, f: (b, i, 0)),
        out_shape=jax.ShapeDtypeStruct((B, L, D), F32),
        scratch_shapes=[pltpu.VMEM((tm, D), BF16), pltpu.VMEM((tm, LANES), F32),
                        pltpu.VMEM((LANES, tm), F32), pltpu.VMEM((LANES, tm), F32),
                        pltpu.VMEM((tm, D), BF16), pltpu.VMEM((tm, D), F32)],
        compiler_params=_cparams(("arbitrary",) * 4),
        name="expert_swiglu",
    )(x, g.reshape(1, D), sc, sh, gate, router, wg, wu, wd)


_P1_Q = 0
_P1_K = 1024
_P1_V = 1280
_P1_QI = 1536
_P1_KD = 2048
_P1_VD = 2560
_P1_KIWI = 3072
_P1_KID = 3200
_P1_N = 3328


def _rope(a, c, s1, s2):
    return a * c + pltpu.roll(a, LANES - ROPE_DIM // 2, 1) * s1 + pltpu.roll(a, ROPE_DIM // 2, 1) * s2


def _proj1_kernel(x_ref, g_ref, sc_ref, sh_ref, w_ref, wvt_ref, c_ref, s1_ref, s2_ref, qg_ref, kg_ref,
                  q_ref, k_ref, v_ref, qi_ref, kd_ref, vd_ref, kiwi_ref, kid_ref, *, vt):
    h = _modnorm(x_ref[...], g_ref[...], sc_ref[...], sh_ref[...]).astype(BF16)
    cos, s1, s2 = c_ref[...], s1_ref[...], s2_ref[...]
    rr = lax.broadcasted_iota(I32, (LANES, LANES), 0) // HEAD_DIM
    cc = lax.broadcasted_iota(I32, (LANES, LANES), 1) // HEAD_DIM
    bd = jnp.where(rr == cc, 1.0, 0.0).astype(BF16)
    lane = lax.broadcasted_iota(I32, (1, LANES), 1)

    def headnorm(a, gain):
        hi, lo = _split2(a * a)
        ss = _dot(hi, bd) + _dot(lo, bd)
        return a * lax.rsqrt(ss * (1.0 / HEAD_DIM) + EPS) * gain

    def group(c0, width):
        y = _dot(h, w_ref[:, c0:c0 + width])
        return [y[:, i * LANES:(i + 1) * LANES] for i in range(width // LANES)]

    qscale = HEAD_DIM ** -0.5 * LOG2E
    for gidx in range(2):
        for i, a in enumerate(group(_P1_Q + gidx * 512, 512)):
            a = _rope(headnorm(a, qg_ref[...]), cos, s1, s2)
            cidx = gidx * 4 + i
            q_ref[:, cidx * LANES:(cidx + 1) * LANES] = (a * qscale).astype(BF16)
    kv = group(_P1_K, 512)
    for i in range(2):
        k_ref[:, i * LANES:(i + 1) * LANES] = _rope(headnorm(kv[i], kg_ref[...]), cos, s1, s2)
        v_ref[:, i * LANES:(i + 1) * LANES] = kv[2 + i]
    for i, a in enumerate(group(_P1_QI, 512)):
        qi_ref[:, i * LANES:(i + 1) * LANES] = (_rope(a, cos, s1, s2) * (HEAD_DIM ** -0.5)).astype(BF16)
    for i, a in enumerate(group(_P1_KD, 512)):
        kd_ref[:, i * LANES:(i + 1) * LANES] = _rope(headnorm(a, kg_ref[...]), cos, s1, s2).astype(BF16)
    if vt:
        vrow = lax.broadcasted_iota(I32, (wvt_ref.shape[0], 1), 0)
        vd_ref[...] = jnp.where(vrow % LANES < HEAD_DIM, _dot_nt(wvt_ref[...], h), 1.0).astype(BF16)
    else:
        for i, a in enumerate(group(_P1_VD, 512)):
            vd_ref[:, i * LANES:(i + 1) * LANES] = a.astype(BF16)
    kiwi, kid = group(_P1_KIWI, 256)
    first = lane < HEAD_DIM
    kiwi = _rope(kiwi, jnp.where(first, cos, 1.0), jnp.where(first, s1, 0.0), jnp.where(first, s2, 0.0))
    is_wi = (lane >= HEAD_DIM) & (lane < HEAD_DIM + IDX_HEADS)
    kiwi_ref[...] = kiwi * jnp.where(is_wi, IDX_HEADS ** -0.5, 1.0)
    kid_ref[...] = _rope(kid, cos, s1, s2).astype(BF16)


def _proj1(x, g, sc, sh, w, rope_tabs, qg, kg, vt):
    B, L, D = x.shape
    tm = _row_tile(L, 512)
    row = lambda d, dt: (pl.BlockSpec((None, tm, d), lambda b, i: (b, i, 0)), jax.ShapeDtypeStruct((B, L, d), dt))
    vd_out = row(512, BF16)
    if vt:
        vd_out = (pl.BlockSpec((None, 512, tm), lambda b, i: (b, 0, i)), jax.ShapeDtypeStruct((B, 512, L), BF16))
    outs = [row(1024, BF16), row(256, F32), row(256, F32), row(512, BF16), row(512, BF16), vd_out,
            row(LANES, F32), row(LANES, BF16)]
    wvt = w[:, _P1_VD:_P1_VD + 512].T
    tab = pl.BlockSpec((tm, LANES), lambda b, i: (i, 0))
    vec = pl.BlockSpec((1, LANES), lambda b, i: (0, 0))
    return pl.pallas_call(
        functools.partial(_proj1_kernel, vt=vt),
        grid=(B, L // tm),
        in_specs=[pl.BlockSpec((None, tm, D), lambda b, i: (b, i, 0)),
                  pl.BlockSpec((1, D), lambda b, i: (0, 0)),
                  _mod_spec(sc, tm), _mod_spec(sh, tm),
                  pl.BlockSpec(w.shape, lambda b, i: (0, 0)),
                  pl.BlockSpec(wvt.shape, lambda b, i: (0, 0)),
                  tab, tab, tab, vec, vec],
        out_specs=[o[0] for o in outs],
        out_shape=[o[1] for o in outs],
        compiler_params=_cparams(("arbitrary", "arbitrary")),
        name="l1_in_proj",
    )(x, g.reshape(1, D), sc, sh, w, wvt, *rope_tabs, qg, kg)


def _pad_l1_weight(w):
    D = w.shape[0]
    q, k, v, qi, ki, wi = jnp.split(w, [1024, 1280, 1536, 2048, 2112], axis=1)
    dup = lambda m, nh: jnp.repeat(m.reshape(D, nh, 1, HEAD_DIM), 2, axis=2).reshape(D, nh * 2 * HEAD_DIM)
    kiwi = jnp.concatenate([ki, wi, jnp.zeros((D, LANES - HEAD_DIM - IDX_HEADS), w.dtype)], axis=1)
    out = jnp.concatenate([q, k, v, qi, dup(k, DSA_KV_HEADS), dup(v, DSA_KV_HEADS), kiwi, dup(ki, 1)], axis=1)
    assert out.shape[1] == _P1_N
    return out.astype(BF16)


def _rope_tables(pos):
    half = ROPE_DIM // 2
    inv = 1.0 / (ROPE_THETA ** (jnp.arange(half, dtype=F32) / half))
    ang = pos.astype(F32)[:, None] * inv[None, :]
    cos, sin = jnp.cos(ang), jnp.sin(ang)
    n = pos.shape[0]
    one = jnp.ones((n, HEAD_DIM - ROPE_DIM), F32)
    zero = jnp.zeros((n, HEAD_DIM - ROPE_DIM), F32)
    z8 = jnp.zeros((n, half), F32)
    c = jnp.concatenate([cos, cos, one], axis=1)
    s1 = jnp.concatenate([-sin, z8, zero], axis=1)
    s2 = jnp.concatenate([z8, sin, zero], axis=1)
    return tuple(jnp.tile(t, (1, 2)) for t in (c, s1, s2))


def _sort_key(score):
    b = lax.bitcast_convert_type(score + 0.0, I32)
    return jnp.where(b < 0, b ^ 0x7FFFFFFF, b)


def _sort_key_const(value):
    b = int(np.float32(value).view(np.int32))
    return b ^ 0x7FFFFFFF if b < 0 else b


SEPARATOR_PASSES = 34


def _topk_kernel(qi_ref, kiwi_ref, kid_ref, mask_ref, key_ref, *, tq, tk, past, s_valid, topk, nkb_max, mask_t):
    i = pl.program_id(1)
    qbase = past + i * tq
    qrow = qbase + lax.broadcasted_iota(I32, (tq, 1), 0)
    adm_lim = jnp.minimum(((qrow >> 6) + 1) << 6, s_valid)
    lim = jnp.minimum((((qbase + tq - 1) >> 6) + 1) << 6, s_valid)
    nkb = jnp.minimum((lim + tk - 1) // tk, nkb_max)
    n_out = (jnp.zeros((tq, 1), I32) + (s_valid - jnp.minimum(nkb * tk, s_valid))).astype(F32)
    lane = lax.broadcasted_iota(I32, (1, LANES), 1)
    half = [lane < HEAD_DIM, lane >= HEAD_DIM]
    col = lax.broadcasted_iota(I32, (tq, tk), 1)
    negkey = _sort_key_const(NEG)
    kiwi = kiwi_ref[...]
    wi = [kiwi[:, HEAD_DIM + h:HEAD_DIM + h + 1] for h in range(IDX_HEADS)]
    qi = qi_ref[...]
    qim = []
    for h in range(IDX_HEADS):
        qc = qi[:, (h // 2) * LANES:(h // 2 + 1) * LANES]
        qim.append(jnp.where(half[h % 2], qc, jnp.zeros_like(qc)))

    n_chunks = tk // LANES
    assert n_chunks >= 2 and topk <= 2 * LANES

    def score_body(j, gmax):
        k0 = pl.multiple_of(j * tk, tk)
        kb = kid_ref[pl.ds(k0, tk), :]
        sc = jnp.zeros((tq, tk), F32)
        for h in range(IDX_HEADS):
            sc = sc + wi[h] * jnp.maximum(_dot_nt(qim[h], kb), 0.0)
        kpos = col + k0
        sc = jnp.where(kpos < adm_lim, sc, NEG)
        key_ref[j] = jnp.where(kpos < s_valid, _sort_key(sc), INT_MIN)
        gmax = list(gmax)
        for c in range(n_chunks):
            gmax[c % 2] = jnp.maximum(gmax[c % 2], sc[:, c * LANES:(c + 1) * LANES])
        return tuple(gmax)

    g_init = jnp.full((tq, LANES), NEG, F32)
    g0, g1 = lax.fori_loop(0, nkb, score_body, (g_init, g_init))
    def count(thr, strict):
        def cbody(j, acc):
            kk = key_ref[j]
            for c in range(tk // LANES):
                kc = kk[:, c * LANES:(c + 1) * LANES]
                hit = (kc > thr) if strict else (kc >= thr)
                acc = acc + jnp.where(hit, 1.0, 0.0)
            return acc
        acc = lax.fori_loop(0, nkb, cbody, jnp.zeros((tq, LANES), F32))
        cnt = jnp.sum(acc, axis=-1, keepdims=True)
        out_hit = (negkey > thr) if strict else (negkey >= thr)
        return cnt + jnp.where(out_hit, n_out, 0.0)

    kf = float(topk)

    def store_mask(j, take):
        kpos = col + j * tk
        mk = jnp.where(kpos < adm_lim, take, 0.0)
        mask_ref[j] = (mk.T if mask_t else mk).astype(mask_ref.dtype)

    s_hi = jnp.max(jnp.maximum(g0, g1), axis=-1, keepdims=True)
    s_lo = jnp.min(jnp.minimum(g0, g1), axis=-1, keepdims=True)
    bounded = jnp.min(jnp.where(s_lo > NEG, 1.0, 0.0), axis=0, keepdims=True).astype(I32)[0, 0] == 1

    def halve(state):
        lo, hi, thr, done = state
        mid = lo + lax.shift_right_logical(hi - lo, 1)
        cnt = count(mid, False)
        live = done == 0
        thr = jnp.where(live & (cnt == kf), mid, thr)
        lo = jnp.where(live & (cnt > kf), mid, lo)
        hi = jnp.where(live & (cnt < kf), mid, hi)
        return lo, hi, thr, jnp.where(cnt == kf, 1, done)

    def search_cond(c):
        return (c[0] < SEPARATOR_PASSES) & jnp.logical_not(c[1])

    def search_body(c):
        st = halve(halve(c[2:]))
        all_done = jnp.min(st[3].astype(F32), axis=0, keepdims=True).astype(I32)[0, 0] == 1
        return (c[0] + 2, all_done) + st

    zero = jnp.zeros((tq, 1), I32)
    found = lax.while_loop(search_cond, search_body,
                           (jnp.where(bounded, 0, SEPARATOR_PASSES), False, _sort_key(s_lo), _sort_key(s_hi), zero, zero))
    separated, thr = found[1], found[4]

    @pl.when(separated)
    def _():
        def sel_body(j, _):
            store_mask(j, jnp.where(key_ref[j] >= thr, 1.0, 0.0))
            return 0

        lax.fori_loop(0, nkb, sel_body, 0)

    @pl.when(jnp.logical_not(separated))
    def _():
        def bis_body(p, prefix):
            cand = prefix | jnp.left_shift(jnp.int32(1), 31 - p)
            cnt = count(cand ^ INT_MIN, False)
            return jnp.where(cnt >= kf, cand, prefix)

        tau = lax.fori_loop(0, 32, bis_body, jnp.zeros((tq, 1), I32)) ^ INT_MIN
        need = kf - count(tau, True)
        rr = lax.broadcasted_iota(I32, (tk, tk), 0)
        cc = lax.broadcasted_iota(I32, (tk, tk), 1)
        U = jnp.where(rr < cc, 1.0, 0.0).astype(BF16)

        def sel_body(j, carry):
            kk = key_ref[j]
            eq = jnp.where(kk == tau, 1.0, 0.0)
            rank = carry + _dot(eq.astype(BF16), U)
            store_mask(j, jnp.where(kk > tau, 1.0, jnp.where(rank < need, eq, 0.0)))
            return carry + jnp.sum(eq, axis=-1, keepdims=True)

        lax.fori_loop(0, nkb, sel_body, jnp.zeros((tq, 1), F32))

    def zero_body(j, _):
        mask_ref[j] = jnp.zeros(mask_ref.shape[1:], mask_ref.dtype)
        return 0

    lax.fori_loop(nkb, nkb_max, zero_body, 0)


def _topk_mask(qi, kiwi, kid, past, s_valid, tq, tk, mask_t):
    B, L, _ = qi.shape
    S = kid.shape[1]
    nb = S // tk
    topk = min(TOPK_MAX, s_valid // 4)
    kern = functools.partial(_topk_kernel, tq=tq, tk=tk, past=past, s_valid=s_valid, topk=topk, nkb_max=nb,
                             mask_t=mask_t)
    mshape = (tk, tq) if mask_t else (tq, tk)
    return pl.pallas_call(
        kern,
        grid=(B, L // tq),
        in_specs=[pl.BlockSpec((None, tq, qi.shape[-1]), lambda b, i: (b, i, 0)),
                  pl.BlockSpec((None, tq, LANES), lambda b, i: (b, i, 0)),
                  pl.BlockSpec((None, S, LANES), lambda b, i: (b, 0, 0))],
        out_specs=pl.BlockSpec((None, None, nb) + mshape, lambda b, i: (b, i, 0, 0, 0)),
        out_shape=jax.ShapeDtypeStruct((B, L // tq, nb) + mshape, BF16),
        scratch_shapes=[pltpu.VMEM((nb, tq, tk), I32)],
        compiler_params=_cparams(("arbitrary", "arbitrary")),
        name="indexer_topk",
    )(qi, kiwi, kid)


DSA_CHAINS = 2
DSA_T_CHAINS = 4


def _dsa_kernel(q_ref, kd_ref, vd_ref, mask_ref, o_ref, m_ref, l_ref, acc_ref, *, tq, tk, past, s_valid, nkb_max):
    i = pl.program_id(1)
    qbase = past + i * tq
    lim = jnp.minimum((((qbase + tq - 1) >> 6) + 1) << 6, s_valid)
    nkb = jnp.minimum((lim + tk - 1) // tk, nkb_max)
    lane = lax.broadcasted_iota(I32, (1, LANES), 1)
    first = lane < HEAD_DIM
    hpp = m_ref.shape[0]
    for j0 in range(0, DSA_KV_HEADS, hpp):
        q4s = []
        for j in range(j0, j0 + hpp):
            qs = []
            for c in (2 * j, 2 * j + 1):
                qc = q_ref[:, c * LANES:(c + 1) * LANES]
                qs += [jnp.where(first, qc, jnp.zeros_like(qc)), jnp.where(first, jnp.zeros_like(qc), qc)]
            q4s.append(jnp.concatenate(qs, axis=0))
        m_ref[...] = jnp.full(m_ref.shape, NEG, F32)
        l_ref[...] = jnp.zeros_like(l_ref)
        acc_ref[...] = jnp.zeros_like(acc_ref)

        def body(jb, _):
            k0 = pl.multiple_of(jb * tk, tk)
            mk = mask_ref[jb].astype(F32)
            sel4 = jnp.concatenate([mk] * 4, axis=0) > 0.5
            for jj in range(hpp):
                j = j0 + jj
                kb = kd_ref[pl.ds(k0, tk), j * LANES:(j + 1) * LANES]
                vb = vd_ref[pl.ds(k0, tk), j * LANES:(j + 1) * LANES]
                s = jnp.where(sel4, _dot_nt(q4s[jj], kb), -jnp.inf)
                m_old = m_ref[jj]
                m_new = jnp.maximum(m_old, jnp.max(s, axis=-1, keepdims=True))
                alpha = jnp.exp2(m_old - m_new)
                p = jnp.exp2(s - m_new)
                l_ref[jj] = alpha * l_ref[jj] + jnp.sum(p, axis=-1, keepdims=True)
                acc_ref[jj] = alpha * acc_ref[jj] + _dot(p.astype(BF16), vb)
                m_ref[jj] = m_new
            return 0

        lax.fori_loop(0, nkb, body, 0)
        for jj in range(hpp):
            o4 = acc_ref[jj] / l_ref[jj]
            for cc in range(2):
                oc = jnp.where(first, o4[(2 * cc) * tq:(2 * cc + 1) * tq], o4[(2 * cc + 1) * tq:(2 * cc + 2) * tq])
                c = 2 * (j0 + jj) + cc
                o_ref[:, c * LANES:(c + 1) * LANES] = oc.astype(o_ref.dtype)


def _dsa_t_kernel(q_ref, kd_ref, vt_ref, mask_ref, o_ref, acc_ref, *, tq, tk, past, s_valid, nkb_max):
    i = pl.program_id(1)
    qbase = past + i * tq
    lim = jnp.minimum((((qbase + tq - 1) >> 6) + 1) << 6, s_valid)
    nkb = jnp.minimum((lim + tk - 1) // tk, nkb_max)
    lane = lax.broadcasted_iota(I32, (1, LANES), 1)
    first = lane < HEAD_DIM
    nch = acc_ref.shape[0]
    pv_rows = HEAD_DIM + 16
    for j0 in range(0, DSA_KV_HEADS, nch):
        q4s = []
        for j in range(j0, j0 + nch):
            qs = []
            for c in (2 * j, 2 * j + 1):
                qc = q_ref[:, c * LANES:(c + 1) * LANES]
                qs += [jnp.where(first, qc, jnp.zeros_like(qc)), jnp.where(first, jnp.zeros_like(qc), qc)]
            q4s.append(jnp.concatenate(qs, axis=0))
        acc_ref[...] = jnp.zeros_like(acc_ref)

        def body(jb, carry):
            k0 = pl.multiple_of(jb * tk, tk)
            nsub = tk // mask_ref.shape[1]
            mk = jnp.concatenate([mask_ref[jb * nsub + t] for t in range(nsub)], axis=0).astype(F32)
            sel4 = jnp.concatenate([mk] * 4, axis=1) > 0.5
            s_raw, p_bf, m_new, alpha = {}, {}, {}, {}

            def qk(jj):
                j = j0 + jj
                kb = kd_ref[pl.ds(k0, tk), j * LANES:(j + 1) * LANES]
                s_raw[jj] = _dot_nt(kb, q4s[jj])

            def softmax(jj):
                s = jnp.where(sel4, s_raw.pop(jj), -jnp.inf)
                m_new[jj] = jnp.maximum(carry[jj], jnp.max(s, axis=0, keepdims=True))
                alpha[jj] = jnp.exp2(carry[jj] - m_new[jj])
                p_bf[jj] = jnp.exp2(s - m_new[jj]).astype(BF16)

            def pv(jj):
                j = j0 + jj
                vb = vt_ref[j * LANES:j * LANES + pv_rows, pl.ds(k0, tk)]
                acc_ref[jj, 0:pv_rows, :] = alpha[jj] * acc_ref[jj, 0:pv_rows, :] + _dot(vb, p_bf.pop(jj))

            qk(0)
            for jj in range(nch):
                if jj + 1 < nch:
                    qk(jj + 1)
                softmax(jj)
                if jj > 0:
                    pv(jj - 1)
            pv(nch - 1)
            return tuple(m_new[jj] for jj in range(nch))

        ms = lax.fori_loop(0, nkb, body, (jnp.full((1, 4 * tq), NEG, F32),) * nch)
        del ms
        for jj in range(nch):
            acc = acc_ref[jj]
            o4 = acc / acc[HEAD_DIM:HEAD_DIM + 1, :]
            heads = [o4[:, g * tq:(g + 1) * tq].T for g in range(4)]
            for cc in range(2):
                c = 2 * (j0 + jj) + cc
                pair = jnp.where(first, heads[2 * cc], pltpu.roll(heads[2 * cc + 1], HEAD_DIM, 1))
                o_ref[:, c * LANES:(c + 1) * LANES] = pair.astype(o_ref.dtype)


def _dsa_attention_t(q, kd, vt, mask_t, past, s_valid, tq, tk):
    B, L, W = q.shape
    S = kd.shape[1]
    nb = S // tk
    assert tq == LANES and tk % mask_t.shape[3] == 0 and S % tk == 0
    kern = functools.partial(_dsa_t_kernel, tq=tq, tk=tk, past=past, s_valid=s_valid, nkb_max=nb)
    return pl.pallas_call(
        kern,
        grid=(B, L // tq),
        in_specs=[pl.BlockSpec((None, tq, W), lambda b, i: (b, i, 0)),
                  pl.BlockSpec((None, S, kd.shape[-1]), lambda b, i: (b, 0, 0)),
                  pl.BlockSpec((None, vt.shape[1], S), lambda b, i: (b, 0, 0)),
                  pl.BlockSpec((None, None) + mask_t.shape[2:], lambda b, i: (b, i, 0, 0, 0))],
        out_specs=pl.BlockSpec((None, tq, W), lambda b, i: (b, i, 0)),
        out_shape=jax.ShapeDtypeStruct((B, L, W), BF16),
        scratch_shapes=[pltpu.VMEM((DSA_T_CHAINS, LANES, 4 * tq), F32)],
        compiler_params=_cparams(("arbitrary", "arbitrary")),
        name="sparse_attn_t",
    )(q, kd, vt, mask_t)


def _dsa_attention(q, kd, vd, mask, past, s_valid, tq, tk):
    B, L, W = q.shape
    S = kd.shape[1]
    nb = S // tk
    kern = functools.partial(_dsa_kernel, tq=tq, tk=tk, past=past, s_valid=s_valid, nkb_max=nb)
    return pl.pallas_call(
        kern,
        grid=(B, L // tq),
        in_specs=[pl.BlockSpec((None, tq, W), lambda b, i: (b, i, 0)),
                  pl.BlockSpec((None, S, kd.shape[-1]), lambda b, i: (b, 0, 0)),
                  pl.BlockSpec((None, S, vd.shape[-1]), lambda b, i: (b, 0, 0)),
                  pl.BlockSpec((None, None, nb, tq, tk), lambda b, i: (b, i, 0, 0, 0))],
        out_specs=pl.BlockSpec((None, tq, W), lambda b, i: (b, i, 0)),
        out_shape=jax.ShapeDtypeStruct((B, L, W), BF16),
        scratch_shapes=[pltpu.VMEM((DSA_CHAINS, 4 * tq, 1), F32), pltpu.VMEM((DSA_CHAINS, 4 * tq, 1), F32),
                        pltpu.VMEM((DSA_CHAINS, 4 * tq, LANES), F32)],
        compiler_params=_cparams(("arbitrary", "arbitrary")),
        name="sparse_attn",
    )(q, kd, vd, mask)


def _pad_rows(a, mult):
    s = a.shape[1]
    sp = -(-s // mult) * mult
    if sp == s:
        return a
    return jnp.concatenate([a, jnp.zeros((a.shape[0], sp - s) + a.shape[2:], a.dtype)], axis=1)


def _dup_heads(a):
    B, S, H, d = a.shape
    return jnp.repeat(a[:, :, :, None, :], 2, axis=3).reshape(B, S, H * 2 * d)


SB_TQ, SB_TK = 512, 256
SB_TK_DECODE = 512
IDX_TK = 512
DSA_TK = 1024


def _trunk(x, mods0, mods1, past, W, flat):
    B, L, D = x.shape
    past_len = 0 if past is None else past[0].shape[1]

    def tok(a):
        return a.reshape(1, B * L, a.shape[-1]) if flat else a

    def untok(a):
        return a.reshape(B, L, a.shape[-1]) if flat else a

    def mod(m):
        if flat:
            return jnp.repeat(m, L, axis=0).reshape(1, B * L, D)
        return m[:, None, :]

    sh_m, sc_m, g_m, sh_f, sc_f, g_f = [mod(m) for m in mods0]
    xt = tok(x)

    q, k, v, kb, vb, u = _proj0(xt, W['l0_norm_mix'], sc_m, sh_m, W['l0_w_in'])
    q, k, v, kb, vb, u = [untok(a) for a in (q, k, v, kb, vb, u)]
    if past is None:
        tq, tk = _row_tile(L, SB_TQ), min(L, SB_TK)
        k_all, v_all = kb, vb
        prev = jnp.zeros((B, 32, u.shape[-1]), F32)
    else:
        tq, tk = L, SB_TK_DECODE
        k_all = _pad_rows(jnp.concatenate([past[0].reshape(B, past_len, -1).astype(BF16), kb], axis=1), tk)
        v_all = _pad_rows(jnp.concatenate([past[1].reshape(B, past_len, -1).astype(BF16), vb], axis=1), tk)
        prev = jnp.concatenate([jnp.zeros((B, 2, u.shape[-1]), F32), past[2]], axis=1)
    o_a = _sb_attention(q, k_all, v_all, past_len, tq, tk)
    o_b = _conv_module(u, prev, W['l0_conv_w'], W['l0_conv_b'], W['l0_conv_ln_g'], W['l0_conv_ln_b'])
    conv_state = jnp.concatenate([prev[:, 2:], u], axis=1)[:, -(CONV_WIDTH - 1):]
    xt = _outproj(xt, g_m, [tok(o_a), tok(o_b)], [W['l0_w_out'][:512], W['l0_w_out'][512:]])
    xt = _ffn(xt, W['l0_norm_ff'], sc_f, sh_f, g_f, W['l0_ff_wg'], W['l0_ff_wu'], W['l0_ff_wd'], 1408)

    sh_m, sc_m, g_m, sh_f, sc_f, g_f = [mod(m) for m in mods1]
    pos = past_len + jnp.arange(L, dtype=I32)
    tabs = _rope_tables(pos)
    if flat:
        tabs = tuple(jnp.tile(t, (B, 1)) for t in tabs)
    qg = jnp.tile(W['l1_q_norm'].reshape(1, HEAD_DIM), (1, 2))
    kg = jnp.tile(W['l1_k_norm'].reshape(1, HEAD_DIM), (1, 2))
    keys_on_sublanes = past is None and L % LANES == 0
    q1, k1, v1, qi, kd, vd, kiwi, kid = _proj1(xt, W['l1_norm_mix'], sc_m, sh_m, W['l1_w_in'], tabs, qg, kg,
                                               vt=keys_on_sublanes)
    q1, k1, v1, qi, kd, kiwi, kid = [untok(a) for a in (q1, k1, v1, qi, kd, kiwi, kid)]
    if not keys_on_sublanes:
        vd = untok(vd)
    s_valid = past_len + L
    tq1 = min(L, LANES)
    tk1 = min(L, IDX_TK) if past is None else IDX_TK
    if past is not None:
        kd = _pad_rows(jnp.concatenate([_dup_heads(past[3]).astype(BF16), kd], axis=1), tk1)
        vd = _pad_rows(jnp.concatenate([_dup_heads(past[4]).astype(BF16), vd], axis=1), tk1)
        kid = _pad_rows(jnp.concatenate([_dup_heads(past[5][:, :, None, :]).astype(BF16), kid], axis=1), tk1)
    mask = _topk_mask(qi, kiwi, kid, past_len, s_valid, tq1, tk1, mask_t=keys_on_sublanes)
    if keys_on_sublanes:
        tk_attn = DSA_TK if kd.shape[1] % DSA_TK == 0 else tk1
        o1 = _dsa_attention_t(q1, kd, vd, mask, past_len, s_valid, tq1, tk_attn)
    else:
        o1 = _dsa_attention(q1, kd, vd, mask, past_len, s_valid, tq1, tk1)
    xt = _outproj(xt, g_m, [tok(o1)], [W['l1_w_out']])
    xt = _moe(xt, W['l1_norm_ff'], sc_f, sh_f, g_f, W['l1_router'], W['l1_exp_wg'], W['l1_exp_wu'],
              W['l1_exp_wd'], 1792)

    hd = HEAD_DIM
    states = (k.reshape(B, L, SB_HEADS, hd), v.reshape(B, L, SB_HEADS, hd), conv_state,
              k1.reshape(B, L, DSA_KV_HEADS, hd), v1.reshape(B, L, DSA_KV_HEADS, hd), kiwi[..., :hd])
    return untok(xt), states


def kernel(x_prompt, x_sample, c_prompt, c_sample, cache_sb_k, cache_sb_v, cache_conv, cache_dsa_k, cache_dsa_v, cache_dsa_kidx, l0_ada_w, l0_ada_b, l0_norm_mix, l0_w_in, l0_conv_w, l0_conv_b, l0_conv_ln_g, l0_conv_ln_b, l0_w_out, l0_norm_ff, l0_ff_wg, l0_ff_wu, l0_ff_wd, l1_ada_w, l1_ada_b, l1_norm_mix, l1_w_in, l1_q_norm, l1_k_norm, l1_w_out, l1_norm_ff, l1_router, l1_exp_wg, l1_exp_wu, l1_exp_wd):
    D = x_prompt.shape[-1]
    router = jnp.concatenate([l1_router, jnp.zeros((D, LANES - N_EXPERTS), F32)], axis=1).astype(BF16)
    W = dict(l0_norm_mix=l0_norm_mix, l0_w_in=l0_w_in.astype(BF16), l0_conv_w=l0_conv_w, l0_conv_b=l0_conv_b,
             l0_conv_ln_g=l0_conv_ln_g, l0_conv_ln_b=l0_conv_ln_b, l0_w_out=l0_w_out.astype(BF16),
             l0_norm_ff=l0_norm_ff, l0_ff_wg=l0_ff_wg.astype(BF16), l0_ff_wu=l0_ff_wu.astype(BF16),
             l0_ff_wd=l0_ff_wd.astype(BF16), l1_norm_mix=l1_norm_mix, l1_w_in=_pad_l1_weight(l1_w_in),
             l1_q_norm=l1_q_norm, l1_k_norm=l1_k_norm, l1_w_out=l1_w_out.astype(BF16), l1_norm_ff=l1_norm_ff,
             l1_router=router, l1_exp_wg=l1_exp_wg.astype(BF16), l1_exp_wu=l1_exp_wu.astype(BF16),
             l1_exp_wd=l1_exp_wd.astype(BF16))
    bp = c_prompt.shape[0]
    c_all = jnp.concatenate([c_prompt, c_sample], axis=0)
    m0 = _ada(c_all, l0_ada_w, l0_ada_b)
    m1 = _ada(c_all, l1_ada_w, l1_ada_b)
    mods = lambda m, sl: [t[sl] for t in jnp.split(m, 6, axis=-1)]
    y_p, st_p = _trunk(x_prompt, mods(m0, slice(0, bp)), mods(m1, slice(0, bp)), None, W, flat=False)
    past = (cache_sb_k, cache_sb_v, cache_conv, cache_dsa_k, cache_dsa_v, cache_dsa_kidx)
    y_s, st_s = _trunk(x_sample, mods(m0, slice(bp, None)), mods(m1, slice(bp, None)), past, W, flat=True)
    return (y_p, y_s) + tuple(st_p) + tuple(st_s)
```

```python
import functools
import math

import numpy as np
import jax
import jax.numpy as jnp
from jax import lax
from jax.experimental import pallas as pl
from jax.experimental.pallas import tpu as pltpu

F32 = jnp.float32
BF16 = jnp.bfloat16
I32 = jnp.int32

EPS = 1e-6
NEG = -1e30
HEAD_DIM = 64
CHUNK = 64
TOPK_MAX = 256
ROPE_DIM = 16
ROPE_THETA = 500000.0
CONV_WIDTH = 31
SB_HEADS = 8
DSA_HEADS = 16
DSA_KV_HEADS = 4
IDX_HEADS = 8
N_EXPERTS = 8
LANES = 128
INT_MIN = -2 ** 31
LOG2E = math.log2(math.e)
VMEM_LIMIT = 56 * 2 ** 20


def _cparams(sem):
    return pltpu.CompilerParams(dimension_semantics=sem, vmem_limit_bytes=VMEM_LIMIT)


def _dot(a, b):
    return jnp.dot(a, b, preferred_element_type=F32)


def _dot_nt(a, b):
    return lax.dot_general(a, b, (((1,), (1,)), ((), ())), preferred_element_type=F32)


def _split2(x):
    hi = x.astype(BF16)
    lo = (x - hi.astype(F32)).astype(BF16)
    return hi, lo


def _sigmoid(x):
    return 1.0 / (1.0 + jnp.exp(-x))


def _modnorm(x, g, sc, sh):
    ms = jnp.mean(x * x, axis=-1, keepdims=True)
    y = x * lax.rsqrt(ms + EPS) * g
    return y * (1.0 + sc) + sh


def _mod_spec(mod, tm):
    if mod.shape[1] == 1:
        return pl.BlockSpec((None, 1, mod.shape[2]), lambda b, i, *_: (b, 0, 0))
    return pl.BlockSpec((None, tm, mod.shape[2]), lambda b, i, *_: (b, i, 0))


def _row_tile(L, pref):
    return pref if L % pref == 0 else L


def _ada_kernel(c_ref, w_ref, b_ref, o_ref):
    c = c_ref[...]
    s = c * _sigmoid(c)
    s_hi, s_lo = _split2(s)
    w_hi, w_lo = _split2(w_ref[...])
    o_ref[...] = _dot(s_hi, w_hi) + _dot(s_lo, w_hi) + _dot(s_hi, w_lo) + b_ref[...]


def _ada(c, w, b):
    bc, d = c.shape
    n = w.shape[1]
    tn = 512
    return pl.pallas_call(
        _ada_kernel,
        grid=(n // tn,),
        in_specs=[pl.BlockSpec((bc, d), lambda j: (0, 0)),
                  pl.BlockSpec((d, tn), lambda j: (0, j)),
                  pl.BlockSpec((1, tn), lambda j: (0, j))],
        out_specs=pl.BlockSpec((bc, tn), lambda j: (0, j)),
        out_shape=jax.ShapeDtypeStruct((bc, n), F32),
        compiler_params=_cparams(("arbitrary",)),
        name="ada_mod",
    )(c, w, b.reshape(1, n))


def _proj0_kernel(x_ref, g_ref, sc_ref, sh_ref, w_ref, q_ref, k_ref, v_ref, kb_ref, vb_ref, u_ref):
    h = _modnorm(x_ref[...], g_ref[...], sc_ref[...], sh_ref[...]).astype(BF16)
    wd = q_ref.shape[-1]

    def mm(c):
        return _dot(h, w_ref[:, c * wd:(c + 1) * wd])

    q_ref[...] = (mm(0) * (HEAD_DIM ** -0.5)).astype(BF16)
    k = mm(1)
    k_ref[...] = k
    kb_ref[...] = k.astype(BF16)
    v = mm(2)
    v_ref[...] = v
    vb_ref[...] = v.astype(BF16)
    u_ref[...] = mm(3) * _sigmoid(mm(4))


def _proj0(x, g, sc, sh, w):
    B, L, D = x.shape
    wd = SB_HEADS * HEAD_DIM
    tm = _row_tile(L, 512)
    row = lambda d, dt: (pl.BlockSpec((None, tm, d), lambda b, i: (b, i, 0)), jax.ShapeDtypeStruct((B, L, d), dt))
    outs = [row(wd, BF16), row(wd, F32), row(wd, F32), row(wd, BF16), row(wd, BF16), row(wd, F32)]
    return pl.pallas_call(
        _proj0_kernel,
        grid=(B, L // tm),
        in_specs=[pl.BlockSpec((None, tm, D), lambda b, i: (b, i, 0)),
                  pl.BlockSpec((1, D), lambda b, i: (0, 0)),
                  _mod_spec(sc, tm), _mod_spec(sh, tm),
                  pl.BlockSpec(w.shape, lambda b, i: (0, 0))],
        out_specs=[o[0] for o in outs],
        out_shape=[o[1] for o in outs],
        compiler_params=_cparams(("arbitrary", "arbitrary")),
        name="l0_in_proj",
    )(x, g.reshape(1, D), sc, sh, w)


SB_EXP_UNDERFLOW = -120.0


def _sb_kernel(q_ref, k_ref, v_ref, o_ref, *, tq, tk, past, nkb_max):
    i = pl.program_id(2)
    qbase = past + i * tq
    nkb = jnp.minimum((qbase + tq - 2 + tk) // tk, nkb_max)
    lane = lax.broadcasted_iota(I32, (1, LANES), 1)
    half = [lane < HEAD_DIM, lane >= HEAD_DIM]
    q = q_ref[...]
    qm = [jnp.where(half[hh], q, jnp.zeros_like(q)) for hh in range(2)]
    rr = lax.broadcasted_iota(I32, (tk, tk), 0)
    cc = lax.broadcasted_iota(I32, (tk, tk), 1)
    U = jnp.where(rr > cc, 1.0, 0.0).astype(BF16)
    dcol = lax.broadcasted_iota(I32, (tq, tk), 1) - lax.broadcasted_iota(I32, (tq, tk), 0)

    def make_body(masked):
        def body(jj, carry):
            r0, r1, acc = carry
            j = nkb - 1 - jj
            k0 = pl.multiple_of(j * tk, tk)
            kb = k_ref[pl.ds(k0, tk), :]
            vb = v_ref[pl.ds(k0, tk), :]
            mask = dcol < (qbase - k0)
            rs = [r0, r1]
            for hh in range(2):
                z = _dot_nt(qm[hh], kb)
                sp = jnp.log(1.0 + jnp.exp(-jnp.abs(z)))
                lb = jnp.minimum(z, 0.0) - sp
                l1m = -jnp.maximum(z, 0.0) - sp
                if masked:
                    l1m = jnp.where(mask, l1m, 0.0)
                hi, lo = _split2(l1m)
                cs = _dot(hi, U) + _dot(lo, U)
                w = jnp.exp(lb + cs + rs[hh])
                if masked:
                    w = jnp.where(mask, w, 0.0)
                vm = jnp.where(half[hh], vb, jnp.zeros_like(vb))
                acc = acc + _dot(w.astype(BF16), vm)
                rs[hh] = rs[hh] + cs[:, 0:1] + l1m[:, 0:1]
            return rs[0], rs[1], acc
        return body

    n_full = jnp.minimum(qbase // tk, nkb)
    z1 = jnp.zeros((tq, 1), F32)
    carry = lax.fori_loop(0, nkb - n_full, make_body(True), (z1, z1, jnp.zeros((tq, LANES), F32)))
    unmasked = make_body(False)

    def live(r0, r1):
        return jnp.max(jnp.maximum(r0, r1)) > SB_EXP_UNDERFLOW

    def w_cond(c):
        return (c[0] < nkb) & c[1]

    def w_body(c):
        r0, r1, acc = unmasked(c[0], c[2:])
        return c[0] + 1, live(r0, r1), r0, r1, acc

    out = lax.while_loop(w_cond, w_body, (nkb - n_full, live(carry[0], carry[1])) + tuple(carry))
    o_ref[...] = out[4].astype(o_ref.dtype)


def _sb_attention(q, k, v, past, tq, tk):
    B, L, W = q.shape
    S = k.shape[1]
    hp = W // LANES
    kern = functools.partial(_sb_kernel, tq=tq, tk=tk, past=past, nkb_max=S // tk)
    return pl.pallas_call(
        kern,
        grid=(B, hp, L // tq),
        in_specs=[pl.BlockSpec((None, tq, LANES), lambda b, h, i: (b, i, h)),
                  pl.BlockSpec((None, S, LANES), lambda b, h, i: (b, 0, h)),
                  pl.BlockSpec((None, S, LANES), lambda b, h, i: (b, 0, h))],
        out_specs=pl.BlockSpec((None, tq, LANES), lambda b, h, i: (b, i, h)),
        out_shape=jax.ShapeDtypeStruct((B, L, W), BF16),
        compiler_params=_cparams(("arbitrary", "arbitrary", "arbitrary")),
        name="stickbreak_attn",
    )(q, k, v)


def _conv_kernel(*refs, tm, has_halo):
    if has_halo:
        u_ref, halo_ref, prev_ref, w_ref, b_ref, g_ref, be_ref, o_ref, buf_ref, cv_ref = refs
    else:
        u_ref, prev_ref, w_ref, b_ref, g_ref, be_ref, o_ref, buf_ref, cv_ref = refs
    i = pl.program_id(1)
    pad = 32
    if has_halo:
        buf_ref[0:pad, :] = jnp.where(i == 0, prev_ref[...], halo_ref[...])
    else:
        buf_ref[0:pad, :] = prev_ref[...]
    buf_ref[pad:pad + tm, :] = u_ref[...]
    C = u_ref.shape[-1]
    rt = min(tm, 64)
    off = pad - (CONV_WIDTH - 1)
    for c in range(C // LANES):
        cs = slice(c * LANES, (c + 1) * LANES)
        for r in range(tm // rt):
            acc = jnp.zeros((rt, LANES), F32)
            for j in range(CONV_WIDTH):
                acc = acc + buf_ref[r * rt + off + j: r * rt + off + j + rt, cs] * w_ref[j:j + 1, cs]
            cv_ref[r * rt:(r + 1) * rt, cs] = acc
    conv = cv_ref[...] + b_ref[...]
    mu = jnp.mean(conv, axis=-1, keepdims=True)
    xc = conv - mu
    var = jnp.mean(xc * xc, axis=-1, keepdims=True)
    y = xc * lax.rsqrt(var + EPS) * g_ref[...] + be_ref[...]
    o_ref[...] = (y * _sigmoid(y)).astype(o_ref.dtype)


def _conv_module(u, prev32, w, b, g, be):
    B, L, C = u.shape
    tm = _row_tile(L, 256)
    has_halo = L > tm
    kern = functools.partial(_conv_kernel, tm=tm, has_halo=has_halo)
    in_specs = [pl.BlockSpec((None, tm, C), lambda b_, i: (b_, i, 0))]
    args = [u]
    if has_halo:
        r = tm // 32
        in_specs.append(pl.BlockSpec((None, 32, C), lambda b_, i: (b_, jnp.maximum(i * r - 1, 0), 0)))
        args.append(u)
    vec = pl.BlockSpec((1, C), lambda b_, i: (0, 0))
    in_specs += [pl.BlockSpec((None, 32, C), lambda b_, i: (b_, 0, 0)),
                 pl.BlockSpec((32, C), lambda b_, i: (0, 0)), vec, vec, vec]
    wpad = jnp.concatenate([w, jnp.zeros((32 - CONV_WIDTH, C), F32)], axis=0)
    args += [prev32, wpad, b.reshape(1, C), g.reshape(1, C), be.reshape(1, C)]
    return pl.pallas_call(
        kern,
        grid=(B, L // tm),
        in_specs=in_specs,
        out_specs=pl.BlockSpec((None, tm, C), lambda b_, i: (b_, i, 0)),
        out_shape=jax.ShapeDtypeStruct((B, L, C), BF16),
        scratch_shapes=[pltpu.VMEM((tm + 32, C), F32), pltpu.VMEM((tm, C), F32)],
        compiler_params=_cparams(("arbitrary", "arbitrary")),
        name="conv_module",
    )(*args)


def _outproj_kernel(*refs, n_in):
    x_ref, g_ref = refs[0], refs[1]
    a_refs = refs[2:2 + n_in]
    w_refs = refs[2 + n_in:2 + 2 * n_in]
    o_ref = refs[2 + 2 * n_in]
    acc = _dot(a_refs[0][...], w_refs[0][...])
    for a_ref, w_ref in zip(a_refs[1:], w_refs[1:]):
        acc = acc + _dot(a_ref[...], w_ref[...])
    o_ref[...] = x_ref[...] + g_ref[...] * acc


def _outproj(x, gate, acts, ws):
    B, L, D = x.shape
    tm = _row_tile(L, 512)
    n_in = len(acts)
    in_specs = [pl.BlockSpec((None, tm, D), lambda b, i: (b, i, 0)), _mod_spec(gate, tm)]
    in_specs += [pl.BlockSpec((None, tm, a.shape[-1]), lambda b, i: (b, i, 0)) for a in acts]
    in_specs += [pl.BlockSpec(w.shape, lambda b, i: (0, 0)) for w in ws]
    return pl.pallas_call(
        functools.partial(_outproj_kernel, n_in=n_in),
        grid=(B, L // tm),
        in_specs=in_specs,
        out_specs=pl.BlockSpec((None, tm, D), lambda b, i: (b, i, 0)),
        out_shape=jax.ShapeDtypeStruct((B, L, D), F32),
        compiler_params=_cparams(("arbitrary", "arbitrary")),
        name="out_proj",
    )(x, gate, *acts, *ws)


def _ffn_kernel(x_ref, g_ref, sc_ref, sh_ref, gate_ref, wg_ref, wu_ref, wd_ref, o_ref, h_ref, acc_ref):
    f = pl.program_id(2)

    @pl.when(f == 0)
    def _():
        h_ref[...] = _modnorm(x_ref[...], g_ref[...], sc_ref[...], sh_ref[...]).astype(BF16)
        acc_ref[...] = jnp.zeros_like(acc_ref)

    h = h_ref[...]
    a = _dot(h, wg_ref[...])
    a = a * _sigmoid(a) * _dot(h, wu_ref[...])
    acc_ref[...] += _dot(a.astype(BF16), wd_ref[...])

    @pl.when(f == pl.num_programs(2) - 1)
    def _():
        o_ref[...] = x_ref[...] + gate_ref[...] * acc_ref[...]


def _ffn(x, g, sc, sh, gate, wg, wu, wd, tf):
    B, L, D = x.shape
    F = wg.shape[1]
    tm = _row_tile(L, 512)
    return pl.pallas_call(
        _ffn_kernel,
        grid=(B, L // tm, F // tf),
        in_specs=[pl.BlockSpec((None, tm, D), lambda b, i, f: (b, i, 0)),
                  pl.BlockSpec((1, D), lambda b, i, f: (0, 0)),
                  _mod_spec(sc, tm), _mod_spec(sh, tm), _mod_spec(gate, tm),
                  pl.BlockSpec((D, tf), lambda b, i, f: (0, f)),
                  pl.BlockSpec((D, tf), lambda b, i, f: (0, f)),
                  pl.BlockSpec((tf, D), lambda b, i, f: (f, 0))],
        out_specs=pl.BlockSpec((None, tm, D), lambda b, i, f: (b, i, 0)),
        out_shape=jax.ShapeDtypeStruct((B, L, D), F32),
        scratch_shapes=[pltpu.VMEM((tm, D), BF16), pltpu.VMEM((tm, D), F32)],
        compiler_params=_cparams(("arbitrary", "arbitrary", "arbitrary")),
        name="dense_swiglu",
    )(x, g.reshape(1, D), sc, sh, gate, wg, wu, wd)


def _dot_tn(a, b):
    return lax.dot_general(a, b, (((0,), (0,)), ((), ())), preferred_element_type=F32)


MOE_GROUP = 128
RANK_BLOCK = 256


def _moe_kernel(x_ref, g_ref, sc_ref, sh_ref, gate_ref, r_ref, wg_ref, wu_ref, wd_ref, o_ref,
                h_ref, rk_ref, rankT_ref, dgT_ref, hc_ref, yc_ref, *, tm):
    e = pl.program_id(2)
    f = pl.program_id(3)
    nf = pl.num_programs(3)
    gs = MOE_GROUP
    lane = lax.broadcasted_iota(I32, (1, LANES), 1)

    @pl.when((e == 0) & (f == 0))
    def _():
        h = _modnorm(x_ref[...], g_ref[...], sc_ref[...], sh_ref[...]).astype(BF16)
        h_ref[...] = h
        o_ref[...] = jnp.zeros_like(o_ref)
        logits = jnp.where(lane < N_EXPERTS, _dot(h, r_ref[...]), -jnp.inf)
        m1 = jnp.max(logits, axis=-1, keepdims=True)
        i1 = jnp.min(jnp.where(logits == m1, lane, LANES), axis=-1, keepdims=True)
        rest = jnp.where(lane == i1, -jnp.inf, logits)
        m2 = jnp.max(rest, axis=-1, keepdims=True)
        i2 = jnp.min(jnp.where(rest == m2, lane, LANES), axis=-1, keepdims=True)
        e2 = jnp.exp(m2 - m1)
        den = 1.0 + e2
        dg = jnp.where(lane == i1, 1.0 / den, 0.0) + jnp.where(lane == i2, e2 / den, 0.0)
        sel = jnp.where(lane == i1, 1.0, 0.0) + jnp.where(lane == i2, 1.0, 0.0)
        rb = min(RANK_BLOCK, tm)
        rr = lax.broadcasted_iota(I32, (rb, rb), 0)
        cc = lax.broadcasted_iota(I32, (rb, rb), 1)
        lower = jnp.where(cc < rr, 1.0, 0.0).astype(BF16)
        carry = jnp.zeros((1, LANES), F32)
        for b in range(tm // rb):
            sb = sel[b * rb:(b + 1) * rb]
            rk_ref[b * rb:(b + 1) * rb, :] = carry + _dot(lower, sb.astype(BF16))
            carry = carry + jnp.sum(sb, axis=0, keepdims=True)
        rank = jnp.where(sel > 0.5, rk_ref[...], -1.0)
        rankT_ref[...] = rank.T
        dgT_ref[...] = dg.T

    rk = rankT_ref[pl.ds(e, 1), :]
    gt = dgT_ref[pl.ds(e, 1), :]
    cnt = jnp.sum(jnp.where(rk >= 0.0, 1.0, 0.0), axis=1, keepdims=True).astype(I32)[0, 0]
    n_groups = (cnt + gs - 1) // gs
    row = lax.broadcasted_iota(I32, (gs, 1), 0)

    def group(gi, _):
        r0 = pl.multiple_of(gi * gs, gs)
        hit = rk == (row + r0).astype(F32)
        p = jnp.where(hit, 1.0, 0.0).astype(BF16)

        @pl.when(f == 0)
        def _():
            hc_ref[pl.ds(r0, gs), :] = _dot(p, h_ref[...]).astype(BF16)

        hg = hc_ref[pl.ds(r0, gs), :]
        a = _dot(hg, wg_ref[...])
        a = a * _sigmoid(a) * _dot(hg, wu_ref[...])
        y = _dot(a.astype(BF16), wd_ref[...])

        @pl.when(f == 0)
        def _():
            yc_ref[pl.ds(r0, gs), :] = y

        @pl.when(f != 0)
        def _():
            yc_ref[pl.ds(r0, gs), :] += y

        @pl.when(f == nf - 1)
        def _():
            gg = jnp.sum(jnp.where(hit, gt, 0.0), axis=1, keepdims=True)
            hi, lo = _split2(yc_ref[pl.ds(r0, gs), :] * gg)
            o_ref[...] += _dot_tn(jnp.concatenate([p, p], axis=0), jnp.concatenate([hi, lo], axis=0))

        return 0

    lax.fori_loop(0, n_groups, group, 0)

    @pl.when((e == pl.num_programs(2) - 1) & (f == nf - 1))
    def _():
        o_ref[...] = x_ref[...] + gate_ref[...] * o_ref[...]


def _moe(x, g, sc, sh, gate, router, wg, wu, wd, tf):
    B, L, D = x.shape
    E, _, F = wg.shape
    tm = _row_tile(L, 1024)
    assert tm % MOE_GROUP == 0
    return pl.pallas_call(
        functools.partial(_moe_kernel, tm=tm),
        grid=(B, L // tm, E, F // tf),
        in_specs=[pl.BlockSpec((None, tm, D), lambda b, i, e, f: (b, i, 0)),
                  pl.BlockSpec((1, D), lambda b, i, e, f: (0, 0)),
                  _mod_spec(sc, tm), _mod_spec(sh, tm), _mod_spec(gate, tm),
                  pl.BlockSpec(router.shape, lambda b, i, e, f: (0, 0)),
                  pl.BlockSpec((None, D, tf), lambda b, i, e, f: (e, 0, f)),
                  pl.BlockSpec((None, D, tf), lambda b, i, e, f: (e, 0, f)),
                  pl.BlockSpec((None, tf, D), lambda b, i, e, f: (e, f, 0))],
        out_specs=pl.BlockSpec((None, tm, D), lambda b, i, e, f: (b, i, 0)),
        out_shape=jax.ShapeDtypeStruct((B, L, D), F32),
        scratch_shapes=[pltpu.VMEM((tm, D), BF16), pltpu.VMEM((tm, LANES), F32),
                        pltpu.VMEM((LANES, tm), F32), pltpu.VMEM((LANES, tm), F32),
                        pltpu.VMEM((tm, D), BF16), pltpu.VMEM((tm, D), F32)],
        compiler_params=_cparams(("arbitrary",) * 4),
        name="expert_swiglu",
    )(x, g.reshape(1, D), sc, sh, gate, router, wg, wu, wd)


_P1_Q = 0
_P1_K = 1024
_P1_V = 1280
_P1_QI = 1536
_P1_KD = 2048
_P1_VD = 2560
_P1_KIWI = 3072
_P1_KID = 3200
_P1_N = 3328


def _rope(a, c, s1, s2):
    return a * c + pltpu.roll(a, LANES - ROPE_DIM // 2, 1) * s1 + pltpu.roll(a, ROPE_DIM // 2, 1) * s2


def _proj1_kernel(x_ref, g_ref, sc_ref, sh_ref, w_ref, wvt_ref, c_ref, s1_ref, s2_ref, qg_ref, kg_ref,
                  q_ref, k_ref, v_ref, qi_ref, kd_ref, vd_ref, kiwi_ref, kid_ref, *, vt):
    h = _modnorm(x_ref[...], g_ref[...], sc_ref[...], sh_ref[...]).astype(BF16)
    cos, s1, s2 = c_ref[...], s1_ref[...], s2_ref[...]
    rr = lax.broadcasted_iota(I32, (LANES, LANES), 0) // HEAD_DIM
    cc = lax.broadcasted_iota(I32, (LANES, LANES), 1) // HEAD_DIM
    bd = jnp.where(rr == cc, 1.0, 0.0).astype(BF16)
    lane = lax.broadcasted_iota(I32, (1, LANES), 1)

    def headnorm(a, gain):
        hi, lo = _split2(a * a)
        ss = _dot(hi, bd) + _dot(lo, bd)
        return a * lax.rsqrt(ss * (1.0 / HEAD_DIM) + EPS) * gain

    def group(c0, width):
        y = _dot(h, w_ref[:, c0:c0 + width])
        return [y[:, i * LANES:(i + 1) * LANES] for i in range(width // LANES)]

    qscale = HEAD_DIM ** -0.5 * LOG2E
    for gidx in range(2):
        for i, a in enumerate(group(_P1_Q + gidx * 512, 512)):
            a = _rope(headnorm(a, qg_ref[...]), cos, s1, s2)
            cidx = gidx * 4 + i
            q_ref[:, cidx * LANES:(cidx + 1) * LANES] = (a * qscale).astype(BF16)
    kv = group(_P1_K, 512)
    for i in range(2):
        k_ref[:, i * LANES:(i + 1) * LANES] = _rope(headnorm(kv[i], kg_ref[...]), cos, s1, s2)
        v_ref[:, i * LANES:(i + 1) * LANES] = kv[2 + i]
    for i, a in enumerate(group(_P1_QI, 512)):
        qi_ref[:, i * LANES:(i + 1) * LANES] = (_rope(a, cos, s1, s2) * (HEAD_DIM ** -0.5)).astype(BF16)
    for i, a in enumerate(group(_P1_KD, 512)):
        kd_ref[:, i * LANES:(i + 1) * LANES] = _rope(headnorm(a, kg_ref[...]), cos, s1, s2).astype(BF16)
    if vt:
        vrow = lax.broadcasted_iota(I32, (wvt_ref.shape[0], 1), 0)
        vd_ref[...] = jnp.where(vrow % LANES < HEAD_DIM, _dot_nt(wvt_ref[...], h), 1.0).astype(BF16)
    else:
        for i, a in enumerate(group(_P1_VD, 512)):
            vd_ref[:, i * LANES:(i + 1) * LANES] = a.astype(BF16)
    kiwi, kid = group(_P1_KIWI, 256)
    first = lane < HEAD_DIM
    kiwi = _rope(kiwi, jnp.where(first, cos, 1.0), jnp.where(first, s1, 0.0), jnp.where(first, s2, 0.0))
    is_wi = (lane >= HEAD_DIM) & (lane < HEAD_DIM + IDX_HEADS)
    kiwi_ref[...] = kiwi * jnp.where(is_wi, IDX_HEADS ** -0.5, 1.0)
    kid_ref[...] = _rope(kid, cos, s1, s2).astype(BF16)


def _proj1(x, g, sc, sh, w, rope_tabs, qg, kg, vt):
    B, L, D = x.shape
    tm = _row_tile(L, 512)
    row = lambda d, dt: (pl.BlockSpec((None, tm, d), lambda b, i: (b, i, 0)), jax.ShapeDtypeStruct((B, L, d), dt))
    vd_out = row(512, BF16)
    if vt:
        vd_out = (pl.BlockSpec((None, 512, tm), lambda b, i: (b, 0, i)), jax.ShapeDtypeStruct((B, 512, L), BF16))
    outs = [row(1024, BF16), row(256, F32), row(256, F32), row(512, BF16), row(512, BF16), vd_out,
            row(LANES, F32), row(LANES, BF16)]
    wvt = w[:, _P1_VD:_P1_VD + 512].T
    tab = pl.BlockSpec((tm, LANES), lambda b, i: (i, 0))
    vec = pl.BlockSpec((1, LANES), lambda b, i: (0, 0))
    return pl.pallas_call(
        functools.partial(_proj1_kernel, vt=vt),
        grid=(B, L // tm),
        in_specs=[pl.BlockSpec((None, tm, D), lambda b, i: (b, i, 0)),
                  pl.BlockSpec((1, D), lambda b, i: (0, 0)),
                  _mod_spec(sc, tm), _mod_spec(sh, tm),
                  pl.BlockSpec(w.shape, lambda b, i: (0, 0)),
                  pl.BlockSpec(wvt.shape, lambda b, i: (0, 0)),
                  tab, tab, tab, vec, vec],
        out_specs=[o[0] for o in outs],
        out_shape=[o[1] for o in outs],
        compiler_params=_cparams(("arbitrary", "arbitrary")),
        name="l1_in_proj",
    )(x, g.reshape(1, D), sc, sh, w, wvt, *rope_tabs, qg, kg)


def _pad_l1_weight(w):
    D = w.shape[0]
    q, k, v, qi, ki, wi = jnp.split(w, [1024, 1280, 1536, 2048, 2112], axis=1)
    dup = lambda m, nh: jnp.repeat(m.reshape(D, nh, 1, HEAD_DIM), 2, axis=2).reshape(D, nh * 2 * HEAD_DIM)
    kiwi = jnp.concatenate([ki, wi, jnp.zeros((D, LANES - HEAD_DIM - IDX_HEADS), w.dtype)], axis=1)
    out = jnp.concatenate([q, k, v, qi, dup(k, DSA_KV_HEADS), dup(v, DSA_KV_HEADS), kiwi, dup(ki, 1)], axis=1)
    assert out.shape[1] == _P1_N
    return out.astype(BF16)


def _rope_tables(pos):
    half = ROPE_DIM // 2
    inv = 1.0 / (ROPE_THETA ** (jnp.arange(half, dtype=F32) / half))
    ang = pos.astype(F32)[:, None] * inv[None, :]
    cos, sin = jnp.cos(ang), jnp.sin(ang)
    n = pos.shape[0]
    one = jnp.ones((n, HEAD_DIM - ROPE_DIM), F32)
    zero = jnp.zeros((n, HEAD_DIM - ROPE_DIM), F32)
    z8 = jnp.zeros((n, half), F32)
    c = jnp.concatenate([cos, cos, one], axis=1)
    s1 = jnp.concatenate([-sin, z8, zero], axis=1)
    s2 = jnp.concatenate([z8, sin, zero], axis=1)
    return tuple(jnp.tile(t, (1, 2)) for t in (c, s1, s2))


def _sort_key(score):
    b = lax.bitcast_convert_type(score + 0.0, I32)
    return jnp.where(b < 0, b ^ 0x7FFFFFFF, b)


def _sort_key_const(value):
    b = int(np.float32(value).view(np.int32))
    return b ^ 0x7FFFFFFF if b < 0 else b


SEARCH_UNROLL = 4
SEARCH_PASSES = 36


def _topk_kernel(qi_ref, kiwi_ref, kid_ref, mask_ref, key_ref, *, tq, tk, past, s_valid, topk, nkb_max, mask_t):
    i = pl.program_id(1)
    qbase = past + i * tq
    qrow = qbase + lax.broadcasted_iota(I32, (tq, 1), 0)
    adm_lim = jnp.minimum(((qrow >> 6) + 1) << 6, s_valid)
    lim = jnp.minimum((((qbase + tq - 1) >> 6) + 1) << 6, s_valid)
    nkb = jnp.minimum((lim + tk - 1) // tk, nkb_max)
    n_out = (jnp.zeros((tq, 1), I32) + (s_valid - jnp.minimum(nkb * tk, s_valid))).astype(F32)
    lane = lax.broadcasted_iota(I32, (1, LANES), 1)
    half = [lane < HEAD_DIM, lane >= HEAD_DIM]
    col = lax.broadcasted_iota(I32, (tq, tk), 1)
    negkey = _sort_key_const(NEG)
    kiwi = kiwi_ref[...]
    wi = [kiwi[:, HEAD_DIM + h:HEAD_DIM + h + 1] for h in range(IDX_HEADS)]
    qi = qi_ref[...]
    qim = []
    for h in range(IDX_HEADS):
        qc = qi[:, (h // 2) * LANES:(h // 2 + 1) * LANES]
        qim.append(jnp.where(half[h % 2], qc, jnp.zeros_like(qc)))

    n_chunks = tk // LANES
    assert n_chunks >= 2 and topk <= 2 * LANES

    def score_body(j, gmax):
        k0 = pl.multiple_of(j * tk, tk)
        kb = kid_ref[pl.ds(k0, tk), :]
        sc = jnp.zeros((tq, tk), F32)
        for h in range(IDX_HEADS):
            sc = sc + wi[h] * jnp.maximum(_dot_nt(qim[h], kb), 0.0)
        kpos = col + k0
        sc = jnp.where(kpos < adm_lim, sc, NEG)
        key_ref[j] = jnp.where(kpos < s_valid, _sort_key(sc), INT_MIN)
        gmax = list(gmax)
        for c in range(n_chunks):
            gmax[c % 2] = jnp.maximum(gmax[c % 2], sc[:, c * LANES:(c + 1) * LANES])
        return tuple(gmax)

    g_init = jnp.full((tq, LANES), NEG, F32)
    g0, g1 = lax.fori_loop(0, nkb, score_body, (g_init, g_init))
    def count(thr, strict):
        def cbody(j, acc):
            kk = key_ref[j]
            for c in range(tk // LANES):
                kc = kk[:, c * LANES:(c + 1) * LANES]
                hit = (kc > thr) if strict else (kc >= thr)
                acc = acc + jnp.where(hit, 1.0, 0.0)
            return acc
        acc = lax.fori_loop(0, nkb, cbody, jnp.zeros((tq, LANES), F32))
        cnt = jnp.sum(acc, axis=-1, keepdims=True)
        out_hit = (negkey > thr) if strict else (negkey >= thr)
        return cnt + jnp.where(out_hit, n_out, 0.0)

    kf = float(topk)

    def store_mask(j, take):
        kpos = col + j * tk
        mk = jnp.where(kpos < adm_lim, take, 0.0)
        mask_ref[j] = (mk.T if mask_t else mk).astype(mask_ref.dtype)

    s_hi = jnp.max(jnp.maximum(g0, g1), axis=-1, keepdims=True)
    s_lo = jnp.min(jnp.minimum(g0, g1), axis=-1, keepdims=True)

    def halve(state):
        lo, hi, chi, thr, done = state
        half = lax.shift_right_logical(hi - lo, 1)
        mid = lo + half
        cnt = count(mid, False)
        live = (done == 0) & (half != 0)
        hit = live & (cnt == kf)
        below = live & (cnt < kf)
        return (jnp.where(live & (cnt > kf), mid, lo), jnp.where(below, mid, hi), jnp.where(below, cnt, chi),
                jnp.where(hit, mid, thr), jnp.where(hit, 1, done))

    def all_rows(flag):
        return jnp.min(flag.astype(F32), axis=0, keepdims=True).astype(I32)[0, 0] == 1

    def search_cond(c):
        return (c[0] < SEARCH_PASSES) & jnp.logical_not(c[1])

    def search_body(c):
        st = c[2:]
        for _ in range(SEARCH_UNROLL):
            st = halve(st)
        lo, hi, _, _, done = st
        ended = jnp.where((done == 1) | (lax.shift_right_logical(hi - lo, 1) == 0), 1, 0)
        return (c[0] + SEARCH_UNROLL, all_rows(ended)) + st

    zero = jnp.zeros((tq, 1), I32)
    found = lax.while_loop(search_cond, search_body,
                           (0, False, _sort_key(s_lo), _sort_key(s_hi) + 1, jnp.zeros((tq, 1), F32), zero, zero))
    lo, chi, thr, done = found[2], found[4], found[5], found[6]
    no_ties = all_rows(done)

    @pl.when(no_ties)
    def _():
        def sel_body(j, _):
            store_mask(j, jnp.where(key_ref[j] >= thr, 1.0, 0.0))
            return 0

        lax.fori_loop(0, nkb, sel_body, 0)

    @pl.when(jnp.logical_not(no_ties))
    def _():
        tau = jnp.where(done == 1, thr, lo)
        need = jnp.where(done == 1, float(tk * nkb_max + 1), kf - chi)
        rr = lax.broadcasted_iota(I32, (tk, tk), 0)
        cc = lax.broadcasted_iota(I32, (tk, tk), 1)
        U = jnp.where(rr < cc, 1.0, 0.0).astype(BF16)

        def sel_body(j, carry):
            kk = key_ref[j]
            eq = jnp.where(kk == tau, 1.0, 0.0)
            rank = carry + _dot(eq.astype(BF16), U)
            store_mask(j, jnp.where(kk > tau, 1.0, jnp.where(rank < need, eq, 0.0)))
            return carry + jnp.sum(eq, axis=-1, keepdims=True)

        lax.fori_loop(0, nkb, sel_body, jnp.zeros((tq, 1), F32))

    def zero_body(j, _):
        mask_ref[j] = jnp.zeros(mask_ref.shape[1:], mask_ref.dtype)
        return 0

    lax.fori_loop(nkb, nkb_max, zero_body, 0)


def _topk_mask(qi, kiwi, kid, past, s_valid, tq, tk, mask_t):
    B, L, _ = qi.shape
    S = kid.shape[1]
    nb = S // tk
    topk = min(TOPK_MAX, s_valid // 4)
    kern = functools.partial(_topk_kernel, tq=tq, tk=tk, past=past, s_valid=s_valid, topk=topk, nkb_max=nb,
                             mask_t=mask_t)
    mshape = (tk, tq) if mask_t else (tq, tk)
    return pl.pallas_call(
        kern,
        grid=(B, L // tq),
        in_specs=[pl.BlockSpec((None, tq, qi.shape[-1]), lambda b, i: (b, i, 0)),
                  pl.BlockSpec((None, tq, LANES), lambda b, i: (b, i, 0)),
                  pl.BlockSpec((None, S, LANES), lambda b, i: (b, 0, 0))],
        out_specs=pl.BlockSpec((None, None, nb) + mshape, lambda b, i: (b, i, 0, 0, 0)),
        out_shape=jax.ShapeDtypeStruct((B, L // tq, nb) + mshape, BF16),
        scratch_shapes=[pltpu.VMEM((nb, tq, tk), I32)],
        compiler_params=_cparams(("arbitrary", "arbitrary")),
        name="indexer_topk",
    )(qi, kiwi, kid)


DSA_CHAINS = 2
DSA_T_CHAINS = 4


def _dsa_kernel(q_ref, kd_ref, vd_ref, mask_ref, o_ref, m_ref, l_ref, acc_ref, *, tq, tk, past, s_valid, nkb_max):
    i = pl.program_id(1)
    qbase = past + i * tq
    lim = jnp.minimum((((qbase + tq - 1) >> 6) + 1) << 6, s_valid)
    nkb = jnp.minimum((lim + tk - 1) // tk, nkb_max)
    lane = lax.broadcasted_iota(I32, (1, LANES), 1)
    first = lane < HEAD_DIM
    hpp = m_ref.shape[0]
    for j0 in range(0, DSA_KV_HEADS, hpp):
        q4s = []
        for j in range(j0, j0 + hpp):
            qs = []
            for c in (2 * j, 2 * j + 1):
                qc = q_ref[:, c * LANES:(c + 1) * LANES]
                qs += [jnp.where(first, qc, jnp.zeros_like(qc)), jnp.where(first, jnp.zeros_like(qc), qc)]
            q4s.append(jnp.concatenate(qs, axis=0))
        m_ref[...] = jnp.full(m_ref.shape, NEG, F32)
        l_ref[...] = jnp.zeros_like(l_ref)
        acc_ref[...] = jnp.zeros_like(acc_ref)

        def body(jb, _):
            k0 = pl.multiple_of(jb * tk, tk)
            mk = mask_ref[jb].astype(F32)
            sel4 = jnp.concatenate([mk] * 4, axis=0) > 0.5
            for jj in range(hpp):
                j = j0 + jj
                kb = kd_ref[pl.ds(k0, tk), j * LANES:(j + 1) * LANES]
                vb = vd_ref[pl.ds(k0, tk), j * LANES:(j + 1) * LANES]
                s = jnp.where(sel4, _dot_nt(q4s[jj], kb), -jnp.inf)
                m_old = m_ref[jj]
                m_new = jnp.maximum(m_old, jnp.max(s, axis=-1, keepdims=True))
                alpha = jnp.exp2(m_old - m_new)
                p = jnp.exp2(s - m_new)
                l_ref[jj] = alpha * l_ref[jj] + jnp.sum(p, axis=-1, keepdims=True)
                acc_ref[jj] = alpha * acc_ref[jj] + _dot(p.astype(BF16), vb)
                m_ref[jj] = m_new
            return 0

        lax.fori_loop(0, nkb, body, 0)
        for jj in range(hpp):
            o4 = acc_ref[jj] / l_ref[jj]
            for cc in range(2):
                oc = jnp.where(first, o4[(2 * cc) * tq:(2 * cc + 1) * tq], o4[(2 * cc + 1) * tq:(2 * cc + 2) * tq])
                c = 2 * (j0 + jj) + cc
                o_ref[:, c * LANES:(c + 1) * LANES] = oc.astype(o_ref.dtype)


def _dsa_t_kernel(q_ref, kd_ref, vt_ref, mask_ref, o_ref, acc_ref, *, tq, tk, past, s_valid, nkb_max):
    i = pl.program_id(1)
    qbase = past + i * tq
    lim = jnp.minimum((((qbase + tq - 1) >> 6) + 1) << 6, s_valid)
    nkb = jnp.minimum((lim + tk - 1) // tk, nkb_max)
    lane = lax.broadcasted_iota(I32, (1, LANES), 1)
    first = lane < HEAD_DIM
    nch = acc_ref.shape[0]
    pv_rows = HEAD_DIM + 16
    for j0 in range(0, DSA_KV_HEADS, nch):
        q4s = []
        for j in range(j0, j0 + nch):
            qs = []
            for c in (2 * j, 2 * j + 1):
                qc = q_ref[:, c * LANES:(c + 1) * LANES]
                qs += [jnp.where(first, qc, jnp.zeros_like(qc)), jnp.where(first, jnp.zeros_like(qc), qc)]
            q4s.append(jnp.concatenate(qs, axis=0))
        acc_ref[...] = jnp.zeros_like(acc_ref)

        def body(jb, carry):
            k0 = pl.multiple_of(jb * tk, tk)
            nsub = tk // mask_ref.shape[1]
            mk = jnp.concatenate([mask_ref[jb * nsub + t] for t in range(nsub)], axis=0).astype(F32)
            sel4 = jnp.concatenate([mk] * 4, axis=1) > 0.5
            s_raw, p_bf, m_new, alpha = {}, {}, {}, {}

            def qk(jj):
                j = j0 + jj
                kb = kd_ref[pl.ds(k0, tk), j * LANES:(j + 1) * LANES]
                s_raw[jj] = _dot_nt(kb, q4s[jj])

            def softmax(jj):
                s = jnp.where(sel4, s_raw.pop(jj), -jnp.inf)
                m_new[jj] = jnp.maximum(carry[jj], jnp.max(s, axis=0, keepdims=True))
                alpha[jj] = jnp.exp2(carry[jj] - m_new[jj])
                p_bf[jj] = jnp.exp2(s - m_new[jj]).astype(BF16)

            def pv(jj):
                j = j0 + jj
                vb = vt_ref[j * LANES:j * LANES + pv_rows, pl.ds(k0, tk)]
                acc_ref[jj, 0:pv_rows, :] = alpha[jj] * acc_ref[jj, 0:pv_rows, :] + _dot(vb, p_bf.pop(jj))

            qk(0)
            for jj in range(nch):
                if jj + 1 < nch:
                    qk(jj + 1)
                softmax(jj)
                if jj > 0:
                    pv(jj - 1)
            pv(nch - 1)
            return tuple(m_new[jj] for jj in range(nch))

        ms = lax.fori_loop(0, nkb, body, (jnp.full((1, 4 * tq), NEG, F32),) * nch)
        del ms
        for jj in range(nch):
            acc = acc_ref[jj]
            o4 = acc / acc[HEAD_DIM:HEAD_DIM + 1, :]
            heads = [o4[:, g * tq:(g + 1) * tq].T for g in range(4)]
            for cc in range(2):
                c = 2 * (j0 + jj) + cc
                pair = jnp.where(first, heads[2 * cc], pltpu.roll(heads[2 * cc + 1], HEAD_DIM, 1))
                o_ref[:, c * LANES:(c + 1) * LANES] = pair.astype(o_ref.dtype)


def _dsa_attention_t(q, kd, vt, mask_t, past, s_valid, tq, tk):
    B, L, W = q.shape
    S = kd.shape[1]
    nb = S // tk
    assert tq == LANES and tk % mask_t.shape[3] == 0 and S % tk == 0
    kern = functools.partial(_dsa_t_kernel, tq=tq, tk=tk, past=past, s_valid=s_valid, nkb_max=nb)
    return pl.pallas_call(
        kern,
        grid=(B, L // tq),
        in_specs=[pl.BlockSpec((None, tq, W), lambda b, i: (b, i, 0)),
                  pl.BlockSpec((None, S, kd.shape[-1]), lambda b, i: (b, 0, 0)),
                  pl.BlockSpec((None, vt.shape[1], S), lambda b, i: (b, 0, 0)),
                  pl.BlockSpec((None, None) + mask_t.shape[2:], lambda b, i: (b, i, 0, 0, 0))],
        out_specs=pl.BlockSpec((None, tq, W), lambda b, i: (b, i, 0)),
        out_shape=jax.ShapeDtypeStruct((B, L, W), BF16),
        scratch_shapes=[pltpu.VMEM((DSA_T_CHAINS, LANES, 4 * tq), F32)],
        compiler_params=_cparams(("arbitrary", "arbitrary")),
        name="sparse_attn_t",
    )(q, kd, vt, mask_t)


def _dsa_attention(q, kd, vd, mask, past, s_valid, tq, tk):
    B, L, W = q.shape
    S = kd.shape[1]
    nb = S // tk
    kern = functools.partial(_dsa_kernel, tq=tq, tk=tk, past=past, s_valid=s_valid, nkb_max=nb)
    return pl.pallas_call(
        kern,
        grid=(B, L // tq),
        in_specs=[pl.BlockSpec((None, tq, W), lambda b, i: (b, i, 0)),
                  pl.BlockSpec((None, S, kd.shape[-1]), lambda b, i: (b, 0, 0)),
                  pl.BlockSpec((None, S, vd.shape[-1]), lambda b, i: (b, 0, 0)),
                  pl.BlockSpec((None, None, nb, tq, tk), lambda b, i: (b, i, 0, 0, 0))],
        out_specs=pl.BlockSpec((None, tq, W), lambda b, i: (b, i, 0)),
        out_shape=jax.ShapeDtypeStruct((B, L, W), BF16),
        scratch_shapes=[pltpu.VMEM((DSA_CHAINS, 4 * tq, 1), F32), pltpu.VMEM((DSA_CHAINS, 4 * tq, 1), F32),
                        pltpu.VMEM((DSA_CHAINS, 4 * tq, LANES), F32)],
        compiler_params=_cparams(("arbitrary", "arbitrary")),
        name="sparse_attn",
    )(q, kd, vd, mask)


def _pad_rows(a, mult):
    s = a.shape[1]
    sp = -(-s // mult) * mult
    if sp == s:
        return a
    return jnp.concatenate([a, jnp.zeros((a.shape[0], sp - s) + a.shape[2:], a.dtype)], axis=1)


def _dup_heads(a):
    B, S, H, d = a.shape
    return jnp.repeat(a[:, :, :, None, :], 2, axis=3).reshape(B, S, H * 2 * d)


SB_TQ, SB_TK = 512, 256
SB_TK_DECODE = 512
IDX_TK = 512
DSA_TK = 1024


def _trunk(x, mods0, mods1, past, W, flat):
    B, L, D = x.shape
    past_len = 0 if past is None else past[0].shape[1]

    def tok(a):
        return a.reshape(1, B * L, a.shape[-1]) if flat else a

    def untok(a):
        return a.reshape(B, L, a.shape[-1]) if flat else a

    def mod(m):
        if flat:
            return jnp.repeat(m, L, axis=0).reshape(1, B * L, D)
        return m[:, None, :]

    sh_m, sc_m, g_m, sh_f, sc_f, g_f = [mod(m) for m in mods0]
    xt = tok(x)

    q, k, v, kb, vb, u = _proj0(xt, W['l0_norm_mix'], sc_m, sh_m, W['l0_w_in'])
    q, k, v, kb, vb, u = [untok(a) for a in (q, k, v, kb, vb, u)]
    if past is None:
        tq, tk = _row_tile(L, SB_TQ), min(L, SB_TK)
        k_all, v_all = kb, vb
        prev = jnp.zeros((B, 32, u.shape[-1]), F32)
    else:
        tq, tk = L, SB_TK_DECODE
        k_all = _pad_rows(jnp.concatenate([past[0].reshape(B, past_len, -1).astype(BF16), kb], axis=1), tk)
        v_all = _pad_rows(jnp.concatenate([past[1].reshape(B, past_len, -1).astype(BF16), vb], axis=1), tk)
        prev = jnp.concatenate([jnp.zeros((B, 2, u.shape[-1]), F32), past[2]], axis=1)
    o_a = _sb_attention(q, k_all, v_all, past_len, tq, tk)
    o_b = _conv_module(u, prev, W['l0_conv_w'], W['l0_conv_b'], W['l0_conv_ln_g'], W['l0_conv_ln_b'])
    conv_state = jnp.concatenate([prev[:, 2:], u], axis=1)[:, -(CONV_WIDTH - 1):]
    xt = _outproj(xt, g_m, [tok(o_a), tok(o_b)], [W['l0_w_out'][:512], W['l0_w_out'][512:]])
    xt = _ffn(xt, W['l0_norm_ff'], sc_f, sh_f, g_f, W['l0_ff_wg'], W['l0_ff_wu'], W['l0_ff_wd'], 1408)

    sh_m, sc_m, g_m, sh_f, sc_f, g_f = [mod(m) for m in mods1]
    pos = past_len + jnp.arange(L, dtype=I32)
    tabs = _rope_tables(pos)
    if flat:
        tabs = tuple(jnp.tile(t, (B, 1)) for t in tabs)
    qg = jnp.tile(W['l1_q_norm'].reshape(1, HEAD_DIM), (1, 2))
    kg = jnp.tile(W['l1_k_norm'].reshape(1, HEAD_DIM), (1, 2))
    keys_on_sublanes = past is None and L % LANES == 0
    q1, k1, v1, qi, kd, vd, kiwi, kid = _proj1(xt, W['l1_norm_mix'], sc_m, sh_m, W['l1_w_in'], tabs, qg, kg,
                                               vt=keys_on_sublanes)
    q1, k1, v1, qi, kd, kiwi, kid = [untok(a) for a in (q1, k1, v1, qi, kd, kiwi, kid)]
    if not keys_on_sublanes:
        vd = untok(vd)
    s_valid = past_len + L
    tq1 = min(L, LANES)
    tk1 = min(L, IDX_TK) if past is None else IDX_TK
    if past is not None:
        kd = _pad_rows(jnp.concatenate([_dup_heads(past[3]).astype(BF16), kd], axis=1), tk1)
        vd = _pad_rows(jnp.concatenate([_dup_heads(past[4]).astype(BF16), vd], axis=1), tk1)
        kid = _pad_rows(jnp.concatenate([_dup_heads(past[5][:, :, None, :]).astype(BF16), kid], axis=1), tk1)
    mask = _topk_mask(qi, kiwi, kid, past_len, s_valid, tq1, tk1, mask_t=keys_on_sublanes)
    if keys_on_sublanes:
        tk_attn = DSA_TK if kd.shape[1] % DSA_TK == 0 else tk1
        o1 = _dsa_attention_t(q1, kd, vd, mask, past_len, s_valid, tq1, tk_attn)
    else:
        o1 = _dsa_attention(q1, kd, vd, mask, past_len, s_valid, tq1, tk1)
    xt = _outproj(xt, g_m, [tok(o1)], [W['l1_w_out']])
    xt = _moe(xt, W['l1_norm_ff'], sc_f, sh_f, g_f, W['l1_router'], W['l1_exp_wg'], W['l1_exp_wu'],
              W['l1_exp_wd'], 1792)

    hd = HEAD_DIM
    states = (k.reshape(B, L, SB_HEADS, hd), v.reshape(B, L, SB_HEADS, hd), conv_state,
              k1.reshape(B, L, DSA_KV_HEADS, hd), v1.reshape(B, L, DSA_KV_HEADS, hd), kiwi[..., :hd])
    return untok(xt), states


def kernel(x_prompt, x_sample, c_prompt, c_sample, cache_sb_k, cache_sb_v, cache_conv, cache_dsa_k, cache_dsa_v, cache_dsa_kidx, l0_ada_w, l0_ada_b, l0_norm_mix, l0_w_in, l0_conv_w, l0_conv_b, l0_conv_ln_g, l0_conv_ln_b, l0_w_out, l0_norm_ff, l0_ff_wg, l0_ff_wu, l0_ff_wd, l1_ada_w, l1_ada_b, l1_norm_mix, l1_w_in, l1_q_norm, l1_k_norm, l1_w_out, l1_norm_ff, l1_router, l1_exp_wg, l1_exp_wu, l1_exp_wd):
    D = x_prompt.shape[-1]
    router = jnp.concatenate([l1_router, jnp.zeros((D, LANES - N_EXPERTS), F32)], axis=1).astype(BF16)
    W = dict(l0_norm_mix=l0_norm_mix, l0_w_in=l0_w_in.astype(BF16), l0_conv_w=l0_conv_w, l0_conv_b=l0_conv_b,
             l0_conv_ln_g=l0_conv_ln_g, l0_conv_ln_b=l0_conv_ln_b, l0_w_out=l0_w_out.astype(BF16),
             l0_norm_ff=l0_norm_ff, l0_ff_wg=l0_ff_wg.astype(BF16), l0_ff_wu=l0_ff_wu.astype(BF16),
             l0_ff_wd=l0_ff_wd.astype(BF16), l1_norm_mix=l1_norm_mix, l1_w_in=_pad_l1_weight(l1_w_in),
             l1_q_norm=l1_q_norm, l1_k_norm=l1_k_norm, l1_w_out=l1_w_out.astype(BF16), l1_norm_ff=l1_norm_ff,
             l1_router=router, l1_exp_wg=l1_exp_wg.astype(BF16), l1_exp_wu=l1_exp_wu.astype(BF16),
             l1_exp_wd=l1_exp_wd.astype(BF16))
    bp = c_prompt.shape[0]
    c_all = jnp.concatenate([c_prompt, c_sample], axis=0)
    m0 = _ada(c_all, l0_ada_w, l0_ada_b)
    m1 = _ada(c_all, l1_ada_w, l1_ada_b)
    mods = lambda m, sl: [t[sl] for t in jnp.split(m, 6, axis=-1)]
    y_p, st_p = _trunk(x_prompt, mods(m0, slice(0, bp)), mods(m1, slice(0, bp)), None, W, flat=False)
    past = (cache_sb_k, cache_sb_v, cache_conv, cache_dsa_k, cache_dsa_v, cache_dsa_kidx)
    y_s, st_s = _trunk(x_sample, mods(m0, slice(bp, None)), mods(m1, slice(bp, None)), past, W, flat=True)
    return (y_p, y_s) + tuple(st_p) + tuple(st_s)
```

```python
import functools
import math

import numpy as np
import jax
import jax.numpy as jnp
from jax import lax
from jax.experimental import pallas as pl
from jax.experimental.pallas import tpu as pltpu

F32 = jnp.float32
BF16 = jnp.bfloat16
I32 = jnp.int32

EPS = 1e-6
NEG = -1e30
HEAD_DIM = 64
CHUNK = 64


CHUNK_SHIFT = CHUNK.bit_length() - 1
assert 1 << CHUNK_SHIFT == CHUNK


def _chunk_end(pos):
    return ((pos >> CHUNK_SHIFT) + 1) << CHUNK_SHIFT
TOPK_MAX = 256
ROPE_DIM = 16
ROPE_THETA = 500000.0
CONV_WIDTH = 31
SB_HEADS = 8
DSA_HEADS = 16
DSA_KV_HEADS = 4
IDX_HEADS = 8
N_EXPERTS = 8
LANES = 128
INT_MIN = -2 ** 31
LOG2E = math.log2(math.e)
VMEM_LIMIT = 56 * 2 ** 20


def _cparams(sem):
    return pltpu.CompilerParams(dimension_semantics=sem, vmem_limit_bytes=VMEM_LIMIT)


def _dot(a, b):
    return jnp.dot(a, b, preferred_element_type=F32)


def _dot_nt(a, b):
    return lax.dot_general(a, b, (((1,), (1,)), ((), ())), preferred_element_type=F32)


def _split2(x):
    hi = x.astype(BF16)
    lo = (x - hi.astype(F32)).astype(BF16)
    return hi, lo


def _sigmoid(x):
    return 1.0 / (1.0 + jnp.exp(-x))


def _modnorm(x, g, sc, sh):
    ms = jnp.mean(x * x, axis=-1, keepdims=True)
    y = x * lax.rsqrt(ms + EPS) * g
    return y * (1.0 + sc) + sh


def _mod_spec(mod, tm):
    if mod.shape[1] == 1:
        return pl.BlockSpec((None, 1, mod.shape[2]), lambda b, i, *_: (b, 0, 0))
    return pl.BlockSpec((None, tm, mod.shape[2]), lambda b, i, *_: (b, i, 0))


def _row_tile(L, pref):
    return pref if L % pref == 0 else L


def _ada_kernel(c_ref, w_ref, b_ref, o_ref):
    c = c_ref[...]
    s = c * _sigmoid(c)
    s_hi, s_lo = _split2(s)
    w_hi, w_lo = _split2(w_ref[...])
    o_ref[...] = _dot(s_hi, w_hi) + _dot(s_lo, w_hi) + _dot(s_hi, w_lo) + b_ref[...]


def _ada(c, w, b):
    bc, d = c.shape
    n = w.shape[1]
    tn = 512
    return pl.pallas_call(
        _ada_kernel,
        grid=(n // tn,),
        in_specs=[pl.BlockSpec((bc, d), lambda j: (0, 0)),
                  pl.BlockSpec((d, tn), lambda j: (0, j)),
                  pl.BlockSpec((1, tn), lambda j: (0, j))],
        out_specs=pl.BlockSpec((bc, tn), lambda j: (0, j)),
        out_shape=jax.ShapeDtypeStruct((bc, n), F32),
        compiler_params=_cparams(("arbitrary",)),
        name="ada_mod",
    )(c, w, b.reshape(1, n))


def _proj0_kernel(x_ref, g_ref, sc_ref, sh_ref, w_ref, q_ref, k_ref, v_ref, kb_ref, vb_ref, u_ref):
    h = _modnorm(x_ref[...], g_ref[...], sc_ref[...], sh_ref[...]).astype(BF16)
    wd = q_ref.shape[-1]

    def mm(c):
        return _dot(h, w_ref[:, c * wd:(c + 1) * wd])

    q_ref[...] = (mm(0) * (HEAD_DIM ** -0.5)).astype(BF16)
    k = mm(1)
    k_ref[...] = k
    kb_ref[...] = k.astype(BF16)
    v = mm(2)
    v_ref[...] = v
    vb_ref[...] = v.astype(BF16)
    u_ref[...] = mm(3) * _sigmoid(mm(4))


def _proj0(x, g, sc, sh, w):
    B, L, D = x.shape
    wd = SB_HEADS * HEAD_DIM
    tm = _row_tile(L, 512)
    row = lambda d, dt: (pl.BlockSpec((None, tm, d), lambda b, i: (b, i, 0)), jax.ShapeDtypeStruct((B, L, d), dt))
    outs = [row(wd, BF16), row(wd, F32), row(wd, F32), row(wd, BF16), row(wd, BF16), row(wd, F32)]
    return pl.pallas_call(
        _proj0_kernel,
        grid=(B, L // tm),
        in_specs=[pl.BlockSpec((None, tm, D), lambda b, i: (b, i, 0)),
                  pl.BlockSpec((1, D), lambda b, i: (0, 0)),
                  _mod_spec(sc, tm), _mod_spec(sh, tm),
                  pl.BlockSpec(w.shape, lambda b, i: (0, 0))],
        out_specs=[o[0] for o in outs],
        out_shape=[o[1] for o in outs],
        compiler_params=_cparams(("arbitrary", "arbitrary")),
        name="l0_in_proj",
    )(x, g.reshape(1, D), sc, sh, w)


SB_EXP_UNDERFLOW = -120.0


def _sb_kernel(q_ref, k_ref, v_ref, o_ref, *, tq, tk, past, nkb_max):
    i = pl.program_id(2)
    qbase = past + i * tq
    nkb = jnp.minimum((qbase + tq - 2 + tk) // tk, nkb_max)
    lane = lax.broadcasted_iota(I32, (1, LANES), 1)
    half = [lane < HEAD_DIM, lane >= HEAD_DIM]
    q = q_ref[...]
    qm = [jnp.where(half[hh], q, jnp.zeros_like(q)) for hh in range(2)]
    rr = lax.broadcasted_iota(I32, (tk, tk), 0)
    cc = lax.broadcasted_iota(I32, (tk, tk), 1)
    U = jnp.where(rr > cc, 1.0, 0.0).astype(BF16)
    dcol = lax.broadcasted_iota(I32, (tq, tk), 1) - lax.broadcasted_iota(I32, (tq, tk), 0)

    def make_body(masked, row0=0):
        def body(jj, carry):
            j = nkb - 1 - jj
            k0 = pl.multiple_of(j * tk, tk)
            kb = k_ref[pl.ds(k0, tk), :]
            vb = v_ref[pl.ds(k0, tk), :]
            mask = dcol[row0:] < (qbase - k0)
            r0, r1, acc = [c[row0:] for c in carry]
            rs = [r0, r1]
            for hh in range(2):
                z = _dot_nt(qm[hh][row0:], kb)
                sp = jnp.log(1.0 + jnp.exp(-jnp.abs(z)))
                lb = jnp.minimum(z, 0.0) - sp
                l1m = -jnp.maximum(z, 0.0) - sp
                if masked:
                    l1m = jnp.where(mask, l1m, 0.0)
                hi, lo = _split2(l1m)
                cs = _dot(hi, U) + _dot(lo, U)
                w = jnp.exp(lb + cs + rs[hh])
                if masked:
                    w = jnp.where(mask, w, 0.0)
                vm = jnp.where(half[hh], vb, jnp.zeros_like(vb))
                acc = acc + _dot(w.astype(BF16), vm)
                rs[hh] = rs[hh] + cs[:, 0:1] + l1m[:, 0:1]
            new = (rs[0], rs[1], acc)
            if row0:
                new = tuple(jnp.concatenate([c[:row0], n], axis=0) for c, n in zip(carry, new))
            return new
        return body

    n_full = jnp.minimum(qbase // tk, nkb)
    z1 = jnp.zeros((tq, 1), F32)
    carry = (z1, z1, jnp.zeros((tq, LANES), F32))
    if past == 0 and tq % tk == 0 and (nkb_max * tk) % tq == 0:
        for jj in range(tq // tk):
            carry = make_body(True, tq - (jj + 1) * tk)(jj, carry)
    else:
        carry = lax.fori_loop(0, nkb - n_full, make_body(True), carry)
    unmasked = make_body(False)

    def live(r0, r1):
        return jnp.max(jnp.maximum(r0, r1)) > SB_EXP_UNDERFLOW

    def w_cond(c):
        return (c[0] < nkb) & c[1]

    def w_body(c):
        r0, r1, acc = unmasked(c[0], c[2:])
        return c[0] + 1, live(r0, r1), r0, r1, acc

    out = lax.while_loop(w_cond, w_body, (nkb - n_full, live(carry[0], carry[1])) + tuple(carry))
    o_ref[...] = out[4].astype(o_ref.dtype)


def _sb_attention(q, k, v, past, tq, tk):
    B, L, W = q.shape
    S = k.shape[1]
    hp = W // LANES
    kern = functools.partial(_sb_kernel, tq=tq, tk=tk, past=past, nkb_max=S // tk)
    return pl.pallas_call(
        kern,
        grid=(B, hp, L // tq),
        in_specs=[pl.BlockSpec((None, tq, LANES), lambda b, h, i: (b, i, h)),
                  pl.BlockSpec((None, S, LANES), lambda b, h, i: (b, 0, h)),
                  pl.BlockSpec((None, S, LANES), lambda b, h, i: (b, 0, h))],
        out_specs=pl.BlockSpec((None, tq, LANES), lambda b, h, i: (b, i, h)),
        out_shape=jax.ShapeDtypeStruct((B, L, W), BF16),
        compiler_params=_cparams(("arbitrary", "arbitrary", "arbitrary")),
        name="stickbreak_attn",
    )(q, k, v)


def _conv_kernel(*refs, tm, has_halo):
    if has_halo:
        u_ref, halo_ref, prev_ref, w_ref, b_ref, g_ref, be_ref, o_ref, buf_ref, cv_ref = refs
    else:
        u_ref, prev_ref, w_ref, b_ref, g_ref, be_ref, o_ref, buf_ref, cv_ref = refs
    i = pl.program_id(1)
    pad = 32
    if has_halo:
        buf_ref[0:pad, :] = jnp.where(i == 0, prev_ref[...], halo_ref[...])
    else:
        buf_ref[0:pad, :] = prev_ref[...]
    buf_ref[pad:pad + tm, :] = u_ref[...]
    C = u_ref.shape[-1]
    rt = min(tm, 64)
    off = pad - (CONV_WIDTH - 1)
    for c in range(C // LANES):
        cs = slice(c * LANES, (c + 1) * LANES)
        for r in range(tm // rt):
            acc = jnp.zeros((rt, LANES), F32)
            for j in range(CONV_WIDTH):
                acc = acc + buf_ref[r * rt + off + j: r * rt + off + j + rt, cs] * w_ref[j:j + 1, cs]
            cv_ref[r * rt:(r + 1) * rt, cs] = acc
    conv = cv_ref[...] + b_ref[...]
    mu = jnp.mean(conv, axis=-1, keepdims=True)
    xc = conv - mu
    var = jnp.mean(xc * xc, axis=-1, keepdims=True)
    y = xc * lax.rsqrt(var + EPS) * g_ref[...] + be_ref[...]
    o_ref[...] = (y * _sigmoid(y)).astype(o_ref.dtype)


def _conv_module(u, prev32, w, b, g, be):
    B, L, C = u.shape
    tm = _row_tile(L, 256)
    has_halo = L > tm
    kern = functools.partial(_conv_kernel, tm=tm, has_halo=has_halo)
    in_specs = [pl.BlockSpec((None, tm, C), lambda b_, i: (b_, i, 0))]
    args = [u]
    if has_halo:
        r = tm // 32
        in_specs.append(pl.BlockSpec((None, 32, C), lambda b_, i: (b_, jnp.maximum(i * r - 1, 0), 0)))
        args.append(u)
    vec = pl.BlockSpec((1, C), lambda b_, i: (0, 0))
    in_specs += [pl.BlockSpec((None, 32, C), lambda b_, i: (b_, 0, 0)),
                 pl.BlockSpec((32, C), lambda b_, i: (0, 0)), vec, vec, vec]
    wpad = jnp.concatenate([w, jnp.zeros((32 - CONV_WIDTH, C), F32)], axis=0)
    args += [prev32, wpad, b.reshape(1, C), g.reshape(1, C), be.reshape(1, C)]
    return pl.pallas_call(
        kern,
        grid=(B, L // tm),
        in_specs=in_specs,
        out_specs=pl.BlockSpec((None, tm, C), lambda b_, i: (b_, i, 0)),
        out_shape=jax.ShapeDtypeStruct((B, L, C), BF16),
        scratch_shapes=[pltpu.VMEM((tm + 32, C), F32), pltpu.VMEM((tm, C), F32)],
        compiler_params=_cparams(("arbitrary", "arbitrary")),
        name="conv_module",
    )(*args)


def _outproj_kernel(*refs, n_in):
    x_ref, g_ref = refs[0], refs[1]
    a_refs = refs[2:2 + n_in]
    w_refs = refs[2 + n_in:2 + 2 * n_in]
    o_ref = refs[2 + 2 * n_in]
    acc = _dot(a_refs[0][...], w_refs[0][...])
    for a_ref, w_ref in zip(a_refs[1:], w_refs[1:]):
        acc = acc + _dot(a_ref[...], w_ref[...])
    o_ref[...] = x_ref[...] + g_ref[...] * acc


def _outproj(x, gate, acts, ws):
    B, L, D = x.shape
    tm = _row_tile(L, 512)
    n_in = len(acts)
    in_specs = [pl.BlockSpec((None, tm, D), lambda b, i: (b, i, 0)), _mod_spec(gate, tm)]
    in_specs += [pl.BlockSpec((None, tm, a.shape[-1]), lambda b, i: (b, i, 0)) for a in acts]
    in_specs += [pl.BlockSpec(w.shape, lambda b, i: (0, 0)) for w in ws]
    return pl.pallas_call(
        functools.partial(_outproj_kernel, n_in=n_in),
        grid=(B, L // tm),
        in_specs=in_specs,
        out_specs=pl.BlockSpec((None, tm, D), lambda b, i: (b, i, 0)),
        out_shape=jax.ShapeDtypeStruct((B, L, D), F32),
        compiler_params=_cparams(("arbitrary", "arbitrary")),
        name="out_proj",
    )(x, gate, *acts, *ws)


def _ffn_kernel(x_ref, g_ref, sc_ref, sh_ref, gate_ref, wg_ref, wu_ref, wd_ref, o_ref, h_ref, acc_ref):
    f = pl.program_id(2)

    @pl.when(f == 0)
    def _():
        h_ref[...] = _modnorm(x_ref[...], g_ref[...], sc_ref[...], sh_ref[...]).astype(BF16)
        acc_ref[...] = jnp.zeros_like(acc_ref)

    h = h_ref[...]
    a = _dot(h, wg_ref[...])
    a = a * _sigmoid(a) * _dot(h, wu_ref[...])
    acc_ref[...] += _dot(a.astype(BF16), wd_ref[...])

    @pl.when(f == pl.num_programs(2) - 1)
    def _():
        o_ref[...] = x_ref[...] + gate_ref[...] * acc_ref[...]


def _ffn(x, g, sc, sh, gate, wg, wu, wd, tf):
    B, L, D = x.shape
    F = wg.shape[1]
    tm = _row_tile(L, 512)
    return pl.pallas_call(
        _ffn_kernel,
        grid=(B, L // tm, F // tf),
        in_specs=[pl.BlockSpec((None, tm, D), lambda b, i, f: (b, i, 0)),
                  pl.BlockSpec((1, D), lambda b, i, f: (0, 0)),
                  _mod_spec(sc, tm), _mod_spec(sh, tm), _mod_spec(gate, tm),
                  pl.BlockSpec((D, tf), lambda b, i, f: (0, f)),
                  pl.BlockSpec((D, tf), lambda b, i, f: (0, f)),
                  pl.BlockSpec((tf, D), lambda b, i, f: (f, 0))],
        out_specs=pl.BlockSpec((None, tm, D), lambda b, i, f: (b, i, 0)),
        out_shape=jax.ShapeDtypeStruct((B, L, D), F32),
        scratch_shapes=[pltpu.VMEM((tm, D), BF16), pltpu.VMEM((tm, D), F32)],
        compiler_params=_cparams(("arbitrary", "arbitrary", "arbitrary")),
        name="dense_swiglu",
    )(x, g.reshape(1, D), sc, sh, gate, wg, wu, wd)


def _dot_tn(a, b):
    return lax.dot_general(a, b, (((0,), (0,)), ((), ())), preferred_element_type=F32)


MOE_GROUP = 128
RANK_BLOCK = 256


def _moe_kernel(x_ref, g_ref, sc_ref, sh_ref, gate_ref, r_ref, wg_ref, wu_ref, wd_ref, o_ref,
                h_ref, rk_ref, rankT_ref, dgT_ref, hc_ref, yc_ref, *, tm):
    e = pl.program_id(2)
    f = pl.program_id(3)
    nf = pl.num_programs(3)
    gs = MOE_GROUP
    lane = lax.broadcasted_iota(I32, (1, LANES), 1)

    @pl.when((e == 0) & (f == 0))
    def _():
        h = _modnorm(x_ref[...], g_ref[...], sc_ref[...], sh_ref[...]).astype(BF16)
        h_ref[...] = h
        o_ref[...] = jnp.zeros_like(o_ref)
        logits = jnp.where(lane < N_EXPERTS, _dot(h, r_ref[...]), -jnp.inf)
        m1 = jnp.max(logits, axis=-1, keepdims=True)
        i1 = jnp.min(jnp.where(logits == m1, lane, LANES), axis=-1, keepdims=True)
        rest = jnp.where(lane == i1, -jnp.inf, logits)
        m2 = jnp.max(rest, axis=-1, keepdims=True)
        i2 = jnp.min(jnp.where(rest == m2, lane, LANES), axis=-1, keepdims=True)
        e2 = jnp.exp(m2 - m1)
        den = 1.0 + e2
        dg = jnp.where(lane == i1, 1.0 / den, 0.0) + jnp.where(lane == i2, e2 / den, 0.0)
        sel = jnp.where(lane == i1, 1.0, 0.0) + jnp.where(lane == i2, 1.0, 0.0)
        rb = min(RANK_BLOCK, tm)
        rr = lax.broadcasted_iota(I32, (rb, rb), 0)
        cc = lax.broadcasted_iota(I32, (rb, rb), 1)
        lower = jnp.where(cc < rr, 1.0, 0.0).astype(BF16)
        carry = jnp.zeros((1, LANES), F32)
        for b in range(tm // rb):
            sb = sel[b * rb:(b + 1) * rb]
            rk_ref[b * rb:(b + 1) * rb, :] = carry + _dot(lower, sb.astype(BF16))
            carry = carry + jnp.sum(sb, axis=0, keepdims=True)
        rank = jnp.where(sel > 0.5, rk_ref[...], -1.0)
        rankT_ref[...] = rank.T
        dgT_ref[...] = dg.T

    rk = rankT_ref[pl.ds(e, 1), :]
    gt = dgT_ref[pl.ds(e, 1), :]
    cnt = jnp.sum(jnp.where(rk >= 0.0, 1.0, 0.0), axis=1, keepdims=True).astype(I32)[0, 0]
    n_groups = (cnt + gs - 1) // gs
    row = lax.broadcasted_iota(I32, (gs, 1), 0)

    def group(gi, _):
        r0 = pl.multiple_of(gi * gs, gs)
        hit = rk == (row + r0).astype(F32)
        p = jnp.where(hit, 1.0, 0.0).astype(BF16)

        @pl.when(f == 0)
        def _():
            hc_ref[pl.ds(r0, gs), :] = _dot(p, h_ref[...]).astype(BF16)

        hg = hc_ref[pl.ds(r0, gs), :]
        a = _dot(hg, wg_ref[...])
        a = a * _sigmoid(a) * _dot(hg, wu_ref[...])
        y = _dot(a.astype(BF16), wd_ref[...])

        @pl.when(f == 0)
        def _():
            yc_ref[pl.ds(r0, gs), :] = y

        @pl.when(f != 0)
        def _():
            yc_ref[pl.ds(r0, gs), :] += y

        @pl.when(f == nf - 1)
        def _():
            gg = jnp.sum(jnp.where(hit, gt, 0.0), axis=1, keepdims=True)
            hi, lo = _split2(yc_ref[pl.ds(r0, gs), :] * gg)
            o_ref[...] += _dot_tn(jnp.concatenate([p, p], axis=0), jnp.concatenate([hi, lo], axis=0))

        return 0

    lax.fori_loop(0, n_groups, group, 0)

    @pl.when((e == pl.num_programs(2) - 1) & (f == nf - 1))
    def _():
        o_ref[...] = x_ref[...] + gate_ref[...] * o_ref[...]


def _moe(x, g, sc, sh, gate, router, wg, wu, wd, tf):
    B, L, D = x.shape
    E, _, F = wg.shape
    tm = _row_tile(L, 1024)
    assert tm % MOE_GROUP == 0
    return pl.pallas_call(
        functools.partial(_moe_kernel, tm=tm),
        grid=(B, L // tm, E, F // tf),
        in_specs=[pl.BlockSpec((None, tm, D), lambda b, i, e, f: (b, i, 0)),
                  pl.BlockSpec((1, D), lambda b, i, e, f: (0, 0)),
                  _mod_spec(sc, tm), _mod_spec(sh, tm), _mod_spec(gate, tm),
                  pl.BlockSpec(router.shape, lambda b, i, e, f: (0, 0)),
                  pl.BlockSpec((None, D, tf), lambda b, i, e, f: (e, 0, f)),
                  pl.BlockSpec((None, D, tf), lambda b, i, e, f: (e, 0, f)),
                  pl.BlockSpec((None, tf, D), lambda b, i, e, f: (e, f, 0))],
        out_specs=pl.BlockSpec((None, tm, D), lambda b, i, e, f: (b, i, 0)),
        out_shape=jax.ShapeDtypeStruct((B, L, D), F32),
        scratch_shapes=[pltpu.VMEM((tm, D), BF16), pltpu.VMEM((tm, LANES), F32),
                        pltpu.VMEM((LANES, tm), F32), pltpu.VMEM((LANES, tm), F32),
                        pltpu.VMEM((tm, D), BF16), pltpu.VMEM((tm, D), F32)],
        compiler_params=_cparams(("arbitrary",) * 4),
        name="expert_swiglu",
    )(x, g.reshape(1, D), sc, sh, gate, router, wg, wu, wd)


_P1_Q = 0
_P1_K = 1024
_P1_V = 1280
_P1_QI = 1536
_P1_KD = 2048
_P1_VD = 2560
_P1_KIWI = 3072
_P1_KID = 3200
_P1_N = 3328


def _rope(a, c, s1, s2):
    return a * c + pltpu.roll(a, LANES - ROPE_DIM // 2, 1) * s1 + pltpu.roll(a, ROPE_DIM // 2, 1) * s2


def _proj1_kernel(x_ref, g_ref, sc_ref, sh_ref, w_ref, wvt_ref, c_ref, s1_ref, s2_ref, qg_ref, kg_ref,
                  q_ref, k_ref, v_ref, qi_ref, kd_ref, vd_ref, kiwi_ref, kid_ref, *, vt):
    h = _modnorm(x_ref[...], g_ref[...], sc_ref[...], sh_ref[...]).astype(BF16)
    cos, s1, s2 = c_ref[...], s1_ref[...], s2_ref[...]
    rr = lax.broadcasted_iota(I32, (LANES, LANES), 0) // HEAD_DIM
    cc = lax.broadcasted_iota(I32, (LANES, LANES), 1) // HEAD_DIM
    bd = jnp.where(rr == cc, 1.0, 0.0).astype(BF16)
    lane = lax.broadcasted_iota(I32, (1, LANES), 1)

    def headnorm(a, gain):
        hi, lo = _split2(a * a)
        ss = _dot(hi, bd) + _dot(lo, bd)
        return a * lax.rsqrt(ss * (1.0 / HEAD_DIM) + EPS) * gain

    def group(c0, width):
        y = _dot(h, w_ref[:, c0:c0 + width])
        return [y[:, i * LANES:(i + 1) * LANES] for i in range(width // LANES)]

    qscale = HEAD_DIM ** -0.5 * LOG2E
    for gidx in range(2):
        for i, a in enumerate(group(_P1_Q + gidx * 512, 512)):
            a = _rope(headnorm(a, qg_ref[...]), cos, s1, s2)
            cidx = gidx * 4 + i
            q_ref[:, cidx * LANES:(cidx + 1) * LANES] = (a * qscale).astype(BF16)
    kv = group(_P1_K, 512)
    for i in range(2):
        k_ref[:, i * LANES:(i + 1) * LANES] = _rope(headnorm(kv[i], kg_ref[...]), cos, s1, s2)
        v_ref[:, i * LANES:(i + 1) * LANES] = kv[2 + i]
    for i, a in enumerate(group(_P1_QI, 512)):
        qi_ref[:, i * LANES:(i + 1) * LANES] = (_rope(a, cos, s1, s2) * (HEAD_DIM ** -0.5)).astype(BF16)
    for i, a in enumerate(group(_P1_KD, 512)):
        kd_ref[:, i * LANES:(i + 1) * LANES] = _rope(headnorm(a, kg_ref[...]), cos, s1, s2).astype(BF16)
    if vt:
        vrow = lax.broadcasted_iota(I32, (wvt_ref.shape[0], 1), 0)
        vd_ref[...] = jnp.where(vrow % LANES < HEAD_DIM, _dot_nt(wvt_ref[...], h), 1.0).astype(BF16)
    else:
        for i, a in enumerate(group(_P1_VD, 512)):
            vd_ref[:, i * LANES:(i + 1) * LANES] = a.astype(BF16)
    kiwi, kid = group(_P1_KIWI, 256)
    first = lane < HEAD_DIM
    kiwi = _rope(kiwi, jnp.where(first, cos, 1.0), jnp.where(first, s1, 0.0), jnp.where(first, s2, 0.0))
    is_wi = (lane >= HEAD_DIM) & (lane < HEAD_DIM + IDX_HEADS)
    kiwi_ref[...] = kiwi * jnp.where(is_wi, IDX_HEADS ** -0.5, 1.0)
    kid_ref[...] = _rope(kid, cos, s1, s2).astype(BF16)


def _proj1(x, g, sc, sh, w, rope_tabs, qg, kg, vt):
    B, L, D = x.shape
    tm = _row_tile(L, 512)
    row = lambda d, dt: (pl.BlockSpec((None, tm, d), lambda b, i: (b, i, 0)), jax.ShapeDtypeStruct((B, L, d), dt))
    vd_out = row(512, BF16)
    if vt:
        vd_out = (pl.BlockSpec((None, 512, tm), lambda b, i: (b, 0, i)), jax.ShapeDtypeStruct((B, 512, L), BF16))
    outs = [row(1024, BF16), row(256, F32), row(256, F32), row(512, BF16), row(512, BF16), vd_out,
            row(LANES, F32), row(LANES, BF16)]
    wvt = w[:, _P1_VD:_P1_VD + 512].T
    tab = pl.BlockSpec((tm, LANES), lambda b, i: (i, 0))
    vec = pl.BlockSpec((1, LANES), lambda b, i: (0, 0))
    return pl.pallas_call(
        functools.partial(_proj1_kernel, vt=vt),
        grid=(B, L // tm),
        in_specs=[pl.BlockSpec((None, tm, D), lambda b, i: (b, i, 0)),
                  pl.BlockSpec((1, D), lambda b, i: (0, 0)),
                  _mod_spec(sc, tm), _mod_spec(sh, tm),
                  pl.BlockSpec(w.shape, lambda b, i: (0, 0)),
                  pl.BlockSpec(wvt.shape, lambda b, i: (0, 0)),
                  tab, tab, tab, vec, vec],
        out_specs=[o[0] for o in outs],
        out_shape=[o[1] for o in outs],
        compiler_params=_cparams(("arbitrary", "arbitrary")),
        name="l1_in_proj",
    )(x, g.reshape(1, D), sc, sh, w, wvt, *rope_tabs, qg, kg)


def _pad_l1_weight(w):
    D = w.shape[0]
    q, k, v, qi, ki, wi = jnp.split(w, [1024, 1280, 1536, 2048, 2112], axis=1)
    dup = lambda m, nh: jnp.repeat(m.reshape(D, nh, 1, HEAD_DIM), 2, axis=2).reshape(D, nh * 2 * HEAD_DIM)
    kiwi = jnp.concatenate([ki, wi, jnp.zeros((D, LANES - HEAD_DIM - IDX_HEADS), w.dtype)], axis=1)
    out = jnp.concatenate([q, k, v, qi, dup(k, DSA_KV_HEADS), dup(v, DSA_KV_HEADS), kiwi, dup(ki, 1)], axis=1)
    assert out.shape[1] == _P1_N
    return out.astype(BF16)


def _rope_tables(pos):
    half = ROPE_DIM // 2
    inv = 1.0 / (ROPE_THETA ** (jnp.arange(half, dtype=F32) / half))
    ang = pos.astype(F32)[:, None] * inv[None, :]
    cos, sin = jnp.cos(ang), jnp.sin(ang)
    n = pos.shape[0]
    one = jnp.ones((n, HEAD_DIM - ROPE_DIM), F32)
    zero = jnp.zeros((n, HEAD_DIM - ROPE_DIM), F32)
    z8 = jnp.zeros((n, half), F32)
    c = jnp.concatenate([cos, cos, one], axis=1)
    s1 = jnp.concatenate([-sin, z8, zero], axis=1)
    s2 = jnp.concatenate([z8, sin, zero], axis=1)
    return tuple(jnp.tile(t, (1, 2)) for t in (c, s1, s2))


def _sort_key(score):
    b = lax.bitcast_convert_type(score + 0.0, I32)
    return jnp.where(b < 0, b ^ 0x7FFFFFFF, b)


def _sort_key_const(value):
    b = int(np.float32(value).view(np.int32))
    return b ^ 0x7FFFFFFF if b < 0 else b


SEARCH_UNROLL = 4
SEARCH_PASSES = 36


def _topk_kernel(qi_ref, kiwi_ref, kid_ref, mask_ref, key_ref, *, tq, tk, past, s_valid, topk, nkb_max, mask_t):
    i = pl.program_id(1)
    qbase = past + i * tq
    qrow = qbase + lax.broadcasted_iota(I32, (tq, 1), 0)
    adm_lim = jnp.minimum(_chunk_end(qrow), s_valid)
    lim = jnp.minimum(_chunk_end(qbase + tq - 1), s_valid)
    nkb = jnp.minimum((lim + tk - 1) // tk, nkb_max)
    n_out = (jnp.zeros((tq, 1), I32) + (s_valid - jnp.minimum(nkb * tk, s_valid))).astype(F32)
    lane = lax.broadcasted_iota(I32, (1, LANES), 1)
    half = [lane < HEAD_DIM, lane >= HEAD_DIM]
    col = lax.broadcasted_iota(I32, (tq, tk), 1)
    negkey = _sort_key_const(NEG)
    kiwi = kiwi_ref[...]
    wi = [kiwi[:, HEAD_DIM + h:HEAD_DIM + h + 1] for h in range(IDX_HEADS)]
    qi = qi_ref[...]
    qim = []
    for h in range(IDX_HEADS):
        qc = qi[:, (h // 2) * LANES:(h // 2 + 1) * LANES]
        qim.append(jnp.where(half[h % 2], qc, jnp.zeros_like(qc)))

    n_chunks = tk // LANES
    assert n_chunks >= 2 and topk <= 2 * LANES

    def score_body(j, gmax):
        k0 = pl.multiple_of(j * tk, tk)
        kb = kid_ref[pl.ds(k0, tk), :]
        sc = jnp.zeros((tq, tk), F32)
        for h in range(IDX_HEADS):
            sc = sc + wi[h] * jnp.maximum(_dot_nt(qim[h], kb), 0.0)
        kpos = col + k0
        sc = jnp.where(kpos < adm_lim, sc, NEG)
        key_ref[j] = jnp.where(kpos < s_valid, _sort_key(sc), INT_MIN)
        gmax = list(gmax)
        for c in range(n_chunks):
            gmax[c % 2] = jnp.maximum(gmax[c % 2], sc[:, c * LANES:(c + 1) * LANES])
        return tuple(gmax)

    g_init = jnp.full((tq, LANES), NEG, F32)
    g0, g1 = lax.fori_loop(0, nkb, score_body, (g_init, g_init))
    def count(thr, strict):
        rows = min(tq, LANES)
        parts = []
        for r0 in range(0, tq, rows):
            thr_r = thr[r0:r0 + rows]

            def cbody(j, acc, r0=r0, thr_r=thr_r):
                for c in range(tk // LANES):
                    kc = key_ref[j, r0:r0 + rows, c * LANES:(c + 1) * LANES]
                    hit = (kc > thr_r) if strict else (kc >= thr_r)
                    acc = acc + jnp.where(hit, 1.0, 0.0)
                return acc
            parts.append(lax.fori_loop(0, nkb, cbody, jnp.zeros((rows, LANES), F32)))
        acc = parts[0] if len(parts) == 1 else jnp.concatenate(parts, axis=0)
        cnt = jnp.sum(acc, axis=-1, keepdims=True)
        out_hit = (negkey > thr) if strict else (negkey >= thr)
        return cnt + jnp.where(out_hit, n_out, 0.0)

    kf = float(topk)

    def store_mask(j, take):
        kpos = col + j * tk
        mk = jnp.where(kpos < adm_lim, take, 0.0)
        if mask_t:
            for h in range(tq // LANES):
                mask_ref[h, j] = mk[h * LANES:(h + 1) * LANES, :].T.astype(mask_ref.dtype)
        else:
            mask_ref[j] = mk.astype(mask_ref.dtype)

    s_hi = jnp.max(jnp.maximum(g0, g1), axis=-1, keepdims=True)
    s_lo = jnp.min(jnp.minimum(g0, g1), axis=-1, keepdims=True)

    def halve(state):
        lo, hi, chi, thr, done = state
        half = lax.shift_right_logical(hi - lo, 1)
        mid = lo + half
        cnt = count(mid, False)
        live = (done == 0) & (half != 0)
        hit = live & (cnt == kf)
        below = live & (cnt < kf)
        return (jnp.where(live & (cnt > kf), mid, lo), jnp.where(below, mid, hi), jnp.where(below, cnt, chi),
                jnp.where(hit, mid, thr), jnp.where(hit, 1, done))

    def all_rows(flag):
        return jnp.min(flag.astype(F32), axis=0, keepdims=True).astype(I32)[0, 0] == 1

    def search_cond(c):
        return (c[0] < SEARCH_PASSES) & jnp.logical_not(c[1])

    def search_body(c):
        st = c[2:]
        for _ in range(SEARCH_UNROLL):
            st = halve(st)
        lo, hi, _, _, done = st
        ended = jnp.where((done == 1) | (lax.shift_right_logical(hi - lo, 1) == 0), 1, 0)
        return (c[0] + SEARCH_UNROLL, all_rows(ended)) + st

    zero = jnp.zeros((tq, 1), I32)
    found = lax.while_loop(search_cond, search_body,
                           (0, False, _sort_key(s_lo), _sort_key(s_hi) + 1, jnp.zeros((tq, 1), F32), zero, zero))
    lo, chi, thr, done = found[2], found[4], found[5], found[6]
    no_ties = all_rows(done)

    @pl.when(no_ties)
    def _():
        def sel_body(j, _):
            store_mask(j, jnp.where(key_ref[j] >= thr, 1.0, 0.0))
            return 0

        lax.fori_loop(0, nkb, sel_body, 0)

    @pl.when(jnp.logical_not(no_ties))
    def _():
        tau = jnp.where(done == 1, thr, lo)
        need = jnp.where(done == 1, float(tk * nkb_max + 1), kf - chi)
        rr = lax.broadcasted_iota(I32, (tk, tk), 0)
        cc = lax.broadcasted_iota(I32, (tk, tk), 1)
        U = jnp.where(rr < cc, 1.0, 0.0).astype(BF16)

        def sel_body(j, carry):
            kk = key_ref[j]
            eq = jnp.where(kk == tau, 1.0, 0.0)
            rank = carry + _dot(eq.astype(BF16), U)
            store_mask(j, jnp.where(kk > tau, 1.0, jnp.where(rank < need, eq, 0.0)))
            return carry + jnp.sum(eq, axis=-1, keepdims=True)

        lax.fori_loop(0, nkb, sel_body, jnp.zeros((tq, 1), F32))

    def zero_body(j, _):
        if mask_t:
            for h in range(tq // LANES):
                mask_ref[h, j] = jnp.zeros(mask_ref.shape[2:], mask_ref.dtype)
        else:
            mask_ref[j] = jnp.zeros(mask_ref.shape[1:], mask_ref.dtype)
        return 0

    lax.fori_loop(nkb, nkb_max, zero_body, 0)


def _topk_mask(qi, kiwi, kid, past, s_valid, tq, tk, mask_t):
    B, L, _ = qi.shape
    S = kid.shape[1]
    nb = S // tk
    topk = min(TOPK_MAX, s_valid // 4)
    kern = functools.partial(_topk_kernel, tq=tq, tk=tk, past=past, s_valid=s_valid, topk=topk, nkb_max=nb,
                             mask_t=mask_t)
    if mask_t:
        assert tq % LANES == 0
        out_spec = pl.BlockSpec((None, tq // LANES, nb, tk, LANES), lambda b, i: (b, i, 0, 0, 0))
        out_shape = jax.ShapeDtypeStruct((B, L // LANES, nb, tk, LANES), BF16)
    else:
        out_spec = pl.BlockSpec((None, None, nb, tq, tk), lambda b, i: (b, i, 0, 0, 0))
        out_shape = jax.ShapeDtypeStruct((B, L // tq, nb, tq, tk), BF16)
    return pl.pallas_call(
        kern,
        grid=(B, L // tq),
        in_specs=[pl.BlockSpec((None, tq, qi.shape[-1]), lambda b, i: (b, i, 0)),
                  pl.BlockSpec((None, tq, LANES), lambda b, i: (b, i, 0)),
                  pl.BlockSpec((None, S, LANES), lambda b, i: (b, 0, 0))],
        out_specs=out_spec,
        out_shape=out_shape,
        scratch_shapes=[pltpu.VMEM((nb, tq, tk), I32)],
        compiler_params=_cparams(("arbitrary", "arbitrary")),
        name="indexer_topk",
    )(qi, kiwi, kid)


DSA_CHAINS = 2
DSA_T_CHAINS = 4


def _dsa_kernel(q_ref, kd_ref, vd_ref, mask_ref, o_ref, m_ref, l_ref, acc_ref, *, tq, tk, past, s_valid, nkb_max):
    i = pl.program_id(1)
    qbase = past + i * tq
    lim = jnp.minimum(_chunk_end(qbase + tq - 1), s_valid)
    nkb = jnp.minimum((lim + tk - 1) // tk, nkb_max)
    lane = lax.broadcasted_iota(I32, (1, LANES), 1)
    first = lane < HEAD_DIM
    hpp = m_ref.shape[0]
    for j0 in range(0, DSA_KV_HEADS, hpp):
        q4s = []
        for j in range(j0, j0 + hpp):
            qs = []
            for c in (2 * j, 2 * j + 1):
                qc = q_ref[:, c * LANES:(c + 1) * LANES]
                qs += [jnp.where(first, qc, jnp.zeros_like(qc)), jnp.where(first, jnp.zeros_like(qc), qc)]
            q4s.append(jnp.concatenate(qs, axis=0))
        m_ref[...] = jnp.full(m_ref.shape, NEG, F32)
        l_ref[...] = jnp.zeros_like(l_ref)
        acc_ref[...] = jnp.zeros_like(acc_ref)

        def body(jb, _):
            k0 = pl.multiple_of(jb * tk, tk)
            mk = mask_ref[jb].astype(F32)
            sel4 = jnp.concatenate([mk] * 4, axis=0) > 0.5
            for jj in range(hpp):
                j = j0 + jj
                kb = kd_ref[pl.ds(k0, tk), j * LANES:(j + 1) * LANES]
                vb = vd_ref[pl.ds(k0, tk), j * LANES:(j + 1) * LANES]
                s = jnp.where(sel4, _dot_nt(q4s[jj], kb), -jnp.inf)
                m_old = m_ref[jj]
                m_new = jnp.maximum(m_old, jnp.max(s, axis=-1, keepdims=True))
                alpha = jnp.exp2(m_old - m_new)
                p = jnp.exp2(s - m_new)
                l_ref[jj] = alpha * l_ref[jj] + jnp.sum(p, axis=-1, keepdims=True)
                acc_ref[jj] = alpha * acc_ref[jj] + _dot(p.astype(BF16), vb)
                m_ref[jj] = m_new
            return 0

        lax.fori_loop(0, nkb, body, 0)
        for jj in range(hpp):
            o4 = acc_ref[jj] / l_ref[jj]
            for cc in range(2):
                oc = jnp.where(first, o4[(2 * cc) * tq:(2 * cc + 1) * tq], o4[(2 * cc + 1) * tq:(2 * cc + 2) * tq])
                c = 2 * (j0 + jj) + cc
                o_ref[:, c * LANES:(c + 1) * LANES] = oc.astype(o_ref.dtype)


def _dsa_t_kernel(q_ref, kd_ref, vt_ref, mask_ref, o_ref, acc_ref, *, tq, tk, past, s_valid, nkb_max):
    i = pl.program_id(1)
    qbase = past + i * tq
    lim = jnp.minimum(_chunk_end(qbase + tq - 1), s_valid)
    nkb = jnp.minimum((lim + tk - 1) // tk, nkb_max)
    lane = lax.broadcasted_iota(I32, (1, LANES), 1)
    first = lane < HEAD_DIM
    nch = acc_ref.shape[0]
    pv_rows = HEAD_DIM + 16
    for j0 in range(0, DSA_KV_HEADS, nch):
        q4s = []
        for j in range(j0, j0 + nch):
            qs = []
            for c in (2 * j, 2 * j + 1):
                qc = q_ref[:, c * LANES:(c + 1) * LANES]
                qs += [jnp.where(first, qc, jnp.zeros_like(qc)), jnp.where(first, jnp.zeros_like(qc), qc)]
            q4s.append(jnp.concatenate(qs, axis=0))
        acc_ref[...] = jnp.zeros_like(acc_ref)

        def body(jb, carry):
            k0 = pl.multiple_of(jb * tk, tk)
            nsub = tk // mask_ref.shape[1]
            mk = jnp.concatenate([mask_ref[jb * nsub + t] for t in range(nsub)], axis=0).astype(F32)
            sel4 = jnp.concatenate([mk] * 4, axis=1) > 0.5
            s_raw, p_bf, m_new, alpha = {}, {}, {}, {}

            def qk(jj):
                j = j0 + jj
                kb = kd_ref[pl.ds(k0, tk), j * LANES:(j + 1) * LANES]
                s_raw[jj] = _dot_nt(kb, q4s[jj])

            def softmax(jj):
                s = jnp.where(sel4, s_raw.pop(jj), -jnp.inf)
                m_new[jj] = jnp.maximum(carry[jj], jnp.max(s, axis=0, keepdims=True))
                alpha[jj] = jnp.exp2(carry[jj] - m_new[jj])
                p_bf[jj] = jnp.exp2(s - m_new[jj]).astype(BF16)

            def pv(jj):
                j = j0 + jj
                vb = vt_ref[j * LANES:j * LANES + pv_rows, pl.ds(k0, tk)]
                acc_ref[jj, 0:pv_rows, :] = alpha[jj] * acc_ref[jj, 0:pv_rows, :] + _dot(vb, p_bf.pop(jj))

            qk(0)
            for jj in range(nch):
                if jj + 1 < nch:
                    qk(jj + 1)
                softmax(jj)
                if jj > 0:
                    pv(jj - 1)
            pv(nch - 1)
            return tuple(m_new[jj] for jj in range(nch))

        lax.fori_loop(0, nkb, body, (jnp.full((1, 4 * tq), NEG, F32),) * nch)
        for jj in range(nch):
            acc = acc_ref[jj]
            o4 = acc / acc[HEAD_DIM:HEAD_DIM + 1, :]
            heads = [o4[:, g * tq:(g + 1) * tq].T for g in range(4)]
            for cc in range(2):
                c = 2 * (j0 + jj) + cc
                pair = jnp.where(first, heads[2 * cc], pltpu.roll(heads[2 * cc + 1], HEAD_DIM, 1))
                o_ref[:, c * LANES:(c + 1) * LANES] = pair.astype(o_ref.dtype)


def _dsa_attention_t(q, kd, vt, mask_t, past, s_valid, tq, tk):
    B, L, W = q.shape
    S = kd.shape[1]
    nb = S // tk
    assert tq == LANES and tk % mask_t.shape[3] == 0 and S % tk == 0
    kern = functools.partial(_dsa_t_kernel, tq=tq, tk=tk, past=past, s_valid=s_valid, nkb_max=nb)
    return pl.pallas_call(
        kern,
        grid=(B, L // tq),
        in_specs=[pl.BlockSpec((None, tq, W), lambda b, i: (b, i, 0)),
                  pl.BlockSpec((None, S, kd.shape[-1]), lambda b, i: (b, 0, 0)),
                  pl.BlockSpec((None, vt.shape[1], S), lambda b, i: (b, 0, 0)),
                  pl.BlockSpec((None, None) + mask_t.shape[2:], lambda b, i: (b, i, 0, 0, 0))],
        out_specs=pl.BlockSpec((None, tq, W), lambda b, i: (b, i, 0)),
        out_shape=jax.ShapeDtypeStruct((B, L, W), BF16),
        scratch_shapes=[pltpu.VMEM((DSA_T_CHAINS, LANES, 4 * tq), F32)],
        compiler_params=_cparams(("arbitrary", "arbitrary")),
        name="sparse_attn_t",
    )(q, kd, vt, mask_t)


def _dsa_attention(q, kd, vd, mask, past, s_valid, tq, tk):
    B, L, W = q.shape
    S = kd.shape[1]
    nb = S // tk
    kern = functools.partial(_dsa_kernel, tq=tq, tk=tk, past=past, s_valid=s_valid, nkb_max=nb)
    return pl.pallas_call(
        kern,
        grid=(B, L // tq),
        in_specs=[pl.BlockSpec((None, tq, W), lambda b, i: (b, i, 0)),
                  pl.BlockSpec((None, S, kd.shape[-1]), lambda b, i: (b, 0, 0)),
                  pl.BlockSpec((None, S, vd.shape[-1]), lambda b, i: (b, 0, 0)),
                  pl.BlockSpec((None, None, nb, tq, tk), lambda b, i: (b, i, 0, 0, 0))],
        out_specs=pl.BlockSpec((None, tq, W), lambda b, i: (b, i, 0)),
        out_shape=jax.ShapeDtypeStruct((B, L, W), BF16),
        scratch_shapes=[pltpu.VMEM((DSA_CHAINS, 4 * tq, 1), F32), pltpu.VMEM((DSA_CHAINS, 4 * tq, 1), F32),
                        pltpu.VMEM((DSA_CHAINS, 4 * tq, LANES), F32)],
        compiler_params=_cparams(("arbitrary", "arbitrary")),
        name="sparse_attn",
    )(q, kd, vd, mask)


def _pad_rows(a, mult):
    s = a.shape[1]
    sp = -(-s // mult) * mult
    if sp == s:
        return a
    return jnp.concatenate([a, jnp.zeros((a.shape[0], sp - s) + a.shape[2:], a.dtype)], axis=1)


def _dup_heads(a):
    B, S, H, d = a.shape
    return jnp.repeat(a[:, :, :, None, :], 2, axis=3).reshape(B, S, H * 2 * d)


SB_TQ, SB_TK = 512, 256
SB_TK_DECODE = 512
IDX_TK = 512
IDX_TQ = 256
DSA_TK = 1024


def _trunk(x, mods0, mods1, past, W, flat):
    B, L, D = x.shape
    past_len = 0 if past is None else past[0].shape[1]

    def tok(a):
        return a.reshape(1, B * L, a.shape[-1]) if flat else a

    def untok(a):
        return a.reshape(B, L, a.shape[-1]) if flat else a

    def mod(m):
        if flat:
            return jnp.repeat(m, L, axis=0).reshape(1, B * L, D)
        return m[:, None, :]

    sh_m, sc_m, g_m, sh_f, sc_f, g_f = [mod(m) for m in mods0]
    xt = tok(x)

    q, k, v, kb, vb, u = _proj0(xt, W['l0_norm_mix'], sc_m, sh_m, W['l0_w_in'])
    q, k, v, kb, vb, u = [untok(a) for a in (q, k, v, kb, vb, u)]
    if past is None:
        tq, tk = _row_tile(L, SB_TQ), min(L, SB_TK)
        k_all, v_all = kb, vb
        prev = jnp.zeros((B, 32, u.shape[-1]), F32)
    else:
        tq, tk = L, SB_TK_DECODE
        k_all = _pad_rows(jnp.concatenate([past[0].reshape(B, past_len, -1).astype(BF16), kb], axis=1), tk)
        v_all = _pad_rows(jnp.concatenate([past[1].reshape(B, past_len, -1).astype(BF16), vb], axis=1), tk)
        prev = jnp.concatenate([jnp.zeros((B, 2, u.shape[-1]), F32), past[2]], axis=1)
    o_a = _sb_attention(q, k_all, v_all, past_len, tq, tk)
    o_b = _conv_module(u, prev, W['l0_conv_w'], W['l0_conv_b'], W['l0_conv_ln_g'], W['l0_conv_ln_b'])
    conv_state = jnp.concatenate([prev[:, 2:], u], axis=1)[:, -(CONV_WIDTH - 1):]
    xt = _outproj(xt, g_m, [tok(o_a), tok(o_b)], [W['l0_w_out'][:512], W['l0_w_out'][512:]])
    xt = _ffn(xt, W['l0_norm_ff'], sc_f, sh_f, g_f, W['l0_ff_wg'], W['l0_ff_wu'], W['l0_ff_wd'], 1408)

    sh_m, sc_m, g_m, sh_f, sc_f, g_f = [mod(m) for m in mods1]
    pos = past_len + jnp.arange(L, dtype=I32)
    tabs = _rope_tables(pos)
    if flat:
        tabs = tuple(jnp.tile(t, (B, 1)) for t in tabs)
    qg = jnp.tile(W['l1_q_norm'].reshape(1, HEAD_DIM), (1, 2))
    kg = jnp.tile(W['l1_k_norm'].reshape(1, HEAD_DIM), (1, 2))
    keys_on_sublanes = past is None and L % LANES == 0
    q1, k1, v1, qi, kd, vd, kiwi, kid = _proj1(xt, W['l1_norm_mix'], sc_m, sh_m, W['l1_w_in'], tabs, qg, kg,
                                               vt=keys_on_sublanes)
    q1, k1, v1, qi, kd, kiwi, kid = [untok(a) for a in (q1, k1, v1, qi, kd, kiwi, kid)]
    if not keys_on_sublanes:
        vd = untok(vd)
    s_valid = past_len + L
    tq1 = min(L, LANES)
    tk1 = min(L, IDX_TK) if past is None else IDX_TK
    if past is not None:
        kd = _pad_rows(jnp.concatenate([_dup_heads(past[3]).astype(BF16), kd], axis=1), tk1)
        vd = _pad_rows(jnp.concatenate([_dup_heads(past[4]).astype(BF16), vd], axis=1), tk1)
        kid = _pad_rows(jnp.concatenate([_dup_heads(past[5][:, :, None, :]).astype(BF16), kid], axis=1), tk1)
    tq_idx = IDX_TQ if keys_on_sublanes and L % IDX_TQ == 0 else tq1
    mask = _topk_mask(qi, kiwi, kid, past_len, s_valid, tq_idx, tk1, mask_t=keys_on_sublanes)
    if keys_on_sublanes:
        tk_attn = DSA_TK if kd.shape[1] % DSA_TK == 0 else tk1
        o1 = _dsa_attention_t(q1, kd, vd, mask, past_len, s_valid, tq1, tk_attn)
    else:
        o1 = _dsa_attention(q1, kd, vd, mask, past_len, s_valid, tq1, tk1)
    xt = _outproj(xt, g_m, [tok(o1)], [W['l1_w_out']])
    xt = _moe(xt, W['l1_norm_ff'], sc_f, sh_f, g_f, W['l1_router'], W['l1_exp_wg'], W['l1_exp_wu'],
              W['l1_exp_wd'], 1792)

    hd = HEAD_DIM
    states = (k.reshape(B, L, SB_HEADS, hd), v.reshape(B, L, SB_HEADS, hd), conv_state,
              k1.reshape(B, L, DSA_KV_HEADS, hd), v1.reshape(B, L, DSA_KV_HEADS, hd), kiwi[..., :hd])
    return untok(xt), states


def kernel(x_prompt, x_sample, c_prompt, c_sample, cache_sb_k, cache_sb_v, cache_conv, cache_dsa_k, cache_dsa_v, cache_dsa_kidx, l0_ada_w, l0_ada_b, l0_norm_mix, l0_w_in, l0_conv_w, l0_conv_b, l0_conv_ln_g, l0_conv_ln_b, l0_w_out, l0_norm_ff, l0_ff_wg, l0_ff_wu, l0_ff_wd, l1_ada_w, l1_ada_b, l1_norm_mix, l1_w_in, l1_q_norm, l1_k_norm, l1_w_out, l1_norm_ff, l1_router, l1_exp_wg, l1_exp_wu, l1_exp_wd):
    D = x_prompt.shape[-1]
    router = jnp.concatenate([l1_router, jnp.zeros((D, LANES - N_EXPERTS), F32)], axis=1).astype(BF16)
    W = dict(l0_norm_mix=l0_norm_mix, l0_w_in=l0_w_in.astype(BF16), l0_conv_w=l0_conv_w, l0_conv_b=l0_conv_b,
             l0_conv_ln_g=l0_conv_ln_g, l0_conv_ln_b=l0_conv_ln_b, l0_w_out=l0_w_out.astype(BF16),
             l0_norm_ff=l0_norm_ff, l0_ff_wg=l0_ff_wg.astype(BF16), l0_ff_wu=l0_ff_wu.astype(BF16),
             l0_ff_wd=l0_ff_wd.astype(BF16), l1_norm_mix=l1_norm_mix, l1_w_in=_pad_l1_weight(l1_w_in),
             l1_q_norm=l1_q_norm, l1_k_norm=l1_k_norm, l1_w_out=l1_w_out.astype(BF16), l1_norm_ff=l1_norm_ff,
             l1_router=router, l1_exp_wg=l1_exp_wg.astype(BF16), l1_exp_wu=l1_exp_wu.astype(BF16),
             l1_exp_wd=l1_exp_wd.astype(BF16))
    bp = c_prompt.shape[0]
    c_all = jnp.concatenate([c_prompt, c_sample], axis=0)
    m0 = _ada(c_all, l0_ada_w, l0_ada_b)
    m1 = _ada(c_all, l1_ada_w, l1_ada_b)
    mods = lambda m, sl: [t[sl] for t in jnp.split(m, 6, axis=-1)]
    y_p, st_p = _trunk(x_prompt, mods(m0, slice(0, bp)), mods(m1, slice(0, bp)), None, W, flat=False)
    past = (cache_sb_k, cache_sb_v, cache_conv, cache_dsa_k, cache_dsa_v, cache_dsa_kidx)
    y_s, st_s = _trunk(x_sample, mods(m0, slice(bp, None)), mods(m1, slice(bp, None)), past, W, flat=True)
    return (y_p, y_s) + tuple(st_p) + tuple(st_s)
```
